```python
import math
import jax, jax.numpy as jnp
from jax import lax
import numpy as np

D_MODEL = 1024
BATCH = 8
SEQ = 2048
DEPTH = 2
DEC_BATCH = 128
DEC_SEQ = 8
PAST_LEN = 16384
PAGE_SIZE = 128

GLA_HEADS = 4
GLA_DK = D_MODEL // 16
GLA_DV = D_MODEL // 8
GLA_LOWRANK = 16
GLA_TAU = 16.0
RET_HEADS = 4
RET_DK = D_MODEL // 16
RET_DV = D_MODEL // 8
ROPE_BASE = 10000.0
MIX_CHUNK = 32
XA_HEADS = 4
XA_DH = D_MODEL // 8
MEM_LEN = 256
D_FF = ((8 * D_MODEL // 3 + 255) // 256) * 256
CONV_W = 3
NORM_EPS = 1e-6
N_BRANCH = 3
SPLIT_SIZES = (GLA_HEADS * GLA_DK, GLA_HEADS * GLA_DK, GLA_HEADS * GLA_DV, GLA_HEADS * GLA_DV, GLA_LOWRANK,
               RET_HEADS * RET_DK, RET_HEADS * RET_DK, RET_HEADS * RET_DV, RET_HEADS * RET_DV,
               XA_HEADS * XA_DH, N_BRANCH * D_MODEL)
IN_COLS = sum(SPLIT_SIZES)

kernel_name = 'gla_retnet_memxattn_convffn_step'


def rms_norm(x, g, eps=NORM_EPS):
    xf = x.astype(jnp.float32)
    y = xf * lax.rsqrt(jnp.mean(xf * xf, axis=-1, keepdims=True) + eps)
    return (y * g.astype(jnp.float32)).astype(x.dtype)


def head_group_norm(x, g, eps=1e-5):
    xf = x.astype(jnp.float32)
    mu = jnp.mean(xf, axis=-1, keepdims=True)
    xc = xf - mu
    var = jnp.mean(xc * xc, axis=-1, keepdims=True)
    return (xc * lax.rsqrt(var + eps) * g.astype(jnp.float32)).astype(x.dtype)


def rope(x, pos):
    half = x.shape[-1] // 2
    inv = ROPE_BASE ** (-jnp.arange(half, dtype=jnp.float32) / half)
    ang = pos.astype(jnp.float32)[:, None] * inv[None, :]
    cos = jnp.cos(ang)[None, :, None, :]
    sin = jnp.sin(ang)[None, :, None, :]
    xf = x.astype(jnp.float32)
    x1, x2 = xf[..., :half], xf[..., half:]
    return jnp.concatenate([x1 * cos - x2 * sin, x1 * sin + x2 * cos], axis=-1).astype(x.dtype)


def chunked_gated_linear_attention(q, k, v, log_a, s0, chunk):
    B, L, H, K = q.shape
    V = v.shape[-1]
    n = L // chunk

    def blocks(t):
        return jnp.moveaxis(t.astype(jnp.float32).reshape(B, n, chunk, H, t.shape[-1]), 1, 0)

    causal = jnp.tril(jnp.ones((chunk, chunk), dtype=bool))

    def step(S, inp):
        qc, kc, vc, lc = inp
        b = jnp.cumsum(lc, axis=1)
        b_last = b[:, -1]
        q_dec = qc * jnp.exp(b)
        k_dec = kc * jnp.exp(-b)
        k_end = kc * jnp.exp(b_last[:, None] - b)
        scores = jnp.where(causal, jnp.einsum('bthk,bshk->bhts', q_dec, k_dec), 0.0)
        o = jnp.einsum('bthk,bhkv->bthv', q_dec, S) + jnp.einsum('bhts,bshv->bthv', scores, vc)
        S = jnp.exp(b_last)[..., None] * S + jnp.einsum('bshk,bshv->bhkv', k_end, vc)
        return S, o

    S, o = lax.scan(step, s0.astype(jnp.float32), (blocks(q), blocks(k), blocks(v), blocks(log_a)))
    o = jnp.moveaxis(o, 0, 1).reshape(B, L, H, V)
    return o.astype(v.dtype), S.astype(s0.dtype)


def memory_kv(mem, g, w_mem_kv):
    B, M, _ = mem.shape
    kv = rms_norm(mem, g) @ w_mem_kv
    k, v = jnp.split(kv, 2, axis=-1)
    return k.reshape(B, M, XA_HEADS, XA_DH), v.reshape(B, M, XA_HEADS, XA_DH)


def memory_attention(q, mem_k, mem_v):
    s = jnp.einsum('blhd,bmhd->bhlm', q, mem_k).astype(jnp.float32) * (XA_DH ** -0.5)
    p = jax.nn.softmax(s, axis=-1).astype(mem_v.dtype)
    return jnp.einsum('bhlm,bmhd->blhd', p, mem_v)


def layer(x, pos, s_gla, s_ret, conv_buf, mem_k, mem_v, p):
    B, L, D = x.shape
    chunk = math.gcd(L, MIX_CHUNK)
    h = rms_norm(x, p['norm1_g'])
    proj = h @ p['w_in']
    split_at = np.cumsum(SPLIT_SIZES)[:-1].tolist()
    gq, gk, gv, gr, ga, rq, rk, rv, rg, xq, gates = jnp.split(proj, split_at, axis=-1)

    log_a = jax.nn.log_sigmoid((ga @ p['gla_a_up'] + p['gla_a_b']).astype(jnp.float32)) / GLA_TAU
    o_gla, s_gla_new = chunked_gated_linear_attention(
        gq.reshape(B, L, GLA_HEADS, GLA_DK) * (GLA_DK ** -0.5),
        gk.reshape(B, L, GLA_HEADS, GLA_DK),
        gv.reshape(B, L, GLA_HEADS, GLA_DV),
        log_a.reshape(B, L, GLA_HEADS, GLA_DK), s_gla, chunk)
    o_gla = rms_norm(o_gla, p['gla_onorm_g'].reshape(GLA_HEADS, GLA_DV)).reshape(B, L, -1) * jax.nn.silu(gr)

    ret_log_decay = jnp.log(1.0 - 2.0 ** (-5.0 - jnp.arange(RET_HEADS, dtype=jnp.float32)))
    log_g = jnp.broadcast_to(ret_log_decay[None, None, :, None], (B, L, RET_HEADS, RET_DK))
    o_ret, s_ret_new = chunked_gated_linear_attention(
        rope(rq.reshape(B, L, RET_HEADS, RET_DK), pos),
        rope(rk.reshape(B, L, RET_HEADS, RET_DK), pos) * (RET_DK ** -0.5),
        rv.reshape(B, L, RET_HEADS, RET_DV), log_g, s_ret, chunk)
    o_ret = head_group_norm(o_ret, p['ret_gnorm_g'].reshape(RET_HEADS, RET_DV)).reshape(B, L, -1) * jax.nn.silu(rg)

    o_xa = memory_attention(xq.reshape(B, L, XA_HEADS, XA_DH), mem_k, mem_v).reshape(B, L, -1)

    gate = jax.nn.sigmoid((gates + p['b_gate']).astype(jnp.float32)).astype(x.dtype).reshape(B, L, N_BRANCH, D)
    merged = (gate[:, :, 0] * (o_gla @ p['w_br_gla'])
              + gate[:, :, 1] * (o_ret @ p['w_br_ret'])
              + gate[:, :, 2] * (o_xa @ p['w_br_xa']))
    x = x + merged @ p['w_out']

    h2 = rms_norm(x, p['norm2_g'])
    a, b = jnp.split(h2 @ p['w_up'], 2, axis=-1)
    a_seq = jnp.concatenate([conv_buf.astype(a.dtype), a], axis=1)
    conv_w = p['conv_w']
    c = p['conv_b'] + sum(a_seq[:, j:j + L] * conv_w[j] for j in range(CONV_W))
    x = x + (jax.nn.gelu(c) * b) @ p['w_down']
    new_buf = a_seq[:, a_seq.shape[1] - (CONV_W - 1):]
    return x, s_gla_new, s_ret_new, new_buf


def setup_inputs(seed: int = 0) -> dict:
    key = jax.random.key(seed)
    ks = jax.random.split(key, 32)
    f32 = jnp.float32

    def nrm(k, shape, scale):
        return jax.random.normal(k, shape, f32) * scale

    def gain(k, shape):
        return 1.0 + 0.02 * jax.random.normal(k, shape, f32)

    return {
        'x_prompt': nrm(ks[0], (BATCH, SEQ, D_MODEL), 1.0),
        'x_sample': nrm(ks[1], (DEC_BATCH, DEC_SEQ, D_MODEL), 1.0),
        'mem_prompt': nrm(ks[2], (BATCH, MEM_LEN, D_MODEL), 1.0),
        'state_gla': nrm(ks[3], (DEPTH, DEC_BATCH, GLA_HEADS, GLA_DK, GLA_DV), 1.0),
        'state_ret': nrm(ks[4], (DEPTH, DEC_BATCH, RET_HEADS, RET_DK, RET_DV), 1.0),
        'state_ffn_conv': nrm(ks[5], (DEPTH, DEC_BATCH, CONV_W - 1, D_FF), 1.0),
        'cache_mem_k': nrm(ks[6], (DEPTH, DEC_BATCH, MEM_LEN, XA_HEADS, XA_DH), 1.0),
        'cache_mem_v': nrm(ks[7], (DEPTH, DEC_BATCH, MEM_LEN, XA_HEADS, XA_DH), 1.0),
        'norm1_g': gain(ks[8], (DEPTH, D_MODEL)),
        'w_in': nrm(ks[9], (DEPTH, D_MODEL, IN_COLS), D_MODEL ** -0.5),
        'gla_a_up': nrm(ks[10], (DEPTH, GLA_LOWRANK, GLA_HEADS * GLA_DK), GLA_LOWRANK ** -0.5),
        'gla_a_b': nrm(ks[11], (DEPTH, GLA_HEADS * GLA_DK), 0.1),
        'gla_onorm_g': gain(ks[12], (DEPTH, GLA_HEADS * GLA_DV)),
        'ret_gnorm_g': gain(ks[13], (DEPTH, RET_HEADS * RET_DV)),
        'mem_norm_g': gain(ks[14], (DEPTH, D_MODEL)),
        'w_mem_kv': nrm(ks[15], (DEPTH, D_MODEL, 2 * XA_HEADS * XA_DH), D_MODEL ** -0.5),
        'w_br_gla': nrm(ks[16], (DEPTH, GLA_HEADS * GLA_DV, D_MODEL), (GLA_HEADS * GLA_DV) ** -0.5),
        'w_br_ret': nrm(ks[17], (DEPTH, RET_HEADS * RET_DV, D_MODEL), (RET_HEADS * RET_DV) ** -0.5),
        'w_br_xa': nrm(ks[18], (DEPTH, XA_HEADS * XA_DH, D_MODEL), (XA_HEADS * XA_DH) ** -0.5),
        'b_gate': nrm(ks[19], (DEPTH, N_BRANCH * D_MODEL), 0.02),
        'w_out': nrm(ks[20], (DEPTH, D_MODEL, D_MODEL), D_MODEL ** -0.5),
        'norm2_g': gain(ks[21], (DEPTH, D_MODEL)),
        'w_up': nrm(ks[22], (DEPTH, D_MODEL, 2 * D_FF), D_MODEL ** -0.5),
        'conv_w': nrm(ks[23], (DEPTH, CONV_W, D_FF), CONV_W ** -0.5),
        'conv_b': nrm(ks[24], (DEPTH, D_FF), 0.02),
        'w_down': nrm(ks[25], (DEPTH, D_FF, D_MODEL), D_FF ** -0.5),
        'final_g': gain(ks[26], (D_MODEL,)),
    }


def reference(x_prompt, x_sample, mem_prompt, state_gla, state_ret, state_ffn_conv, cache_mem_k, cache_mem_v,
              norm1_g, w_in, gla_a_up, gla_a_b, gla_onorm_g, ret_gnorm_g, mem_norm_g, w_mem_kv,
              w_br_gla, w_br_ret, w_br_xa, b_gate, w_out, norm2_g, w_up, conv_w, conv_b, w_down, final_g):
    Bp, Lp, _ = x_prompt.shape
    Ls = x_sample.shape[1]
    pos_p = jnp.arange(Lp, dtype=jnp.int32)
    pos_s = PAST_LEN + jnp.arange(Ls, dtype=jnp.int32)
    dt = x_prompt.dtype
    xp, xs = x_prompt, x_sample
    p_gla, p_ret, p_conv, p_mk, p_mv = [], [], [], [], []
    s_gla, s_ret, s_conv = [], [], []
    for i in range(DEPTH):
        p = {'norm1_g': norm1_g[i], 'w_in': w_in[i], 'gla_a_up': gla_a_up[i], 'gla_a_b': gla_a_b[i],
             'gla_onorm_g': gla_onorm_g[i], 'ret_gnorm_g': ret_gnorm_g[i], 'w_br_gla': w_br_gla[i],
             'w_br_ret': w_br_ret[i], 'w_br_xa': w_br_xa[i], 'b_gate': b_gate[i], 'w_out': w_out[i],
             'norm2_g': norm2_g[i], 'w_up': w_up[i], 'conv_w': conv_w[i], 'conv_b': conv_b[i], 'w_down': w_down[i]}
        mk, mv = memory_kv(mem_prompt, mem_norm_g[i], w_mem_kv[i])
        xp, g_new, r_new, c_new = layer(
            xp, pos_p,
            jnp.zeros((Bp, GLA_HEADS, GLA_DK, GLA_DV), dt),
            jnp.zeros((Bp, RET_HEADS, RET_DK, RET_DV), dt),
            jnp.zeros((Bp, CONV_W - 1, D_FF), dt), mk, mv, p)
        p_gla.append(g_new); p_ret.append(r_new); p_conv.append(c_new); p_mk.append(mk); p_mv.append(mv)
        xs, g_new, r_new, c_new = layer(
            xs, pos_s, state_gla[i], state_ret[i], state_ffn_conv[i], cache_mem_k[i], cache_mem_v[i], p)
        s_gla.append(g_new); s_ret.append(r_new); s_conv.append(c_new)
    y_prompt = rms_norm(xp, final_g)
    y_sample = rms_norm(xs, final_g)
    p_state_gla = jnp.stack(p_gla)
    p_state_ret = jnp.stack(p_ret)
    p_state_ffn_conv = jnp.stack(p_conv)
    p_cache_mem_k = jnp.stack(p_mk)
    p_cache_mem_v = jnp.stack(p_mv)
    s_state_gla = jnp.stack(s_gla)
    s_state_ret = jnp.stack(s_ret)
    s_state_ffn_conv = jnp.stack(s_conv)
    return (y_prompt, y_sample, p_state_gla, p_state_ret, p_state_ffn_conv, p_cache_mem_k, p_cache_mem_v,
            s_state_gla, s_state_ret, s_state_ffn_conv)
```

```python
import functools
import math

import jax
import jax.numpy as jnp
import numpy as np
from jax import lax
from jax.experimental import pallas as pl
from jax.experimental.pallas import tpu as pltpu

F32 = jnp.float32
BF16 = jnp.bfloat16

D_MODEL = 1024
DEPTH = 2
PAST_LEN = 16384
HEADS = 4
DK = 64
DV = 128
QK_W = HEADS * DK
V_W = HEADS * DV
GLA_LOWRANK = 16
GLA_TAU = 16.0
ROPE_BASE = 10000.0
MIX_CHUNK = 32
MEM_LEN = 256
D_FF = 2816
CONV_W = 3
NORM_EPS = 1e-6
GROUP_NORM_EPS = 1e-5
N_BRANCH = 3

LANES = 128
VMEM_LIMIT_BYTES = 56 * 1024 * 1024

OFF_GQ = 0
OFF_GK = OFF_GQ + QK_W
OFF_GV = OFF_GK + QK_W
OFF_GR = OFF_GV + V_W
OFF_RQ = OFF_GR + V_W
OFF_RK = OFF_RQ + QK_W
OFF_RV = OFF_RK + QK_W
OFF_RG = OFF_RV + V_W
OFF_XQ = OFF_RG + V_W
OFF_GATES = OFF_XQ + V_W
OFF_GA = OFF_GATES + N_BRANCH * D_MODEL
IN_COLS_PACKED = OFF_GA + LANES
GA_SRC = 2 * QK_W + 2 * V_W

PROMPT_TILE = 256
SAMPLE_SEQS = 8


def _dot(a, b):
    return jnp.dot(a, b, preferred_element_type=F32)


def _dot_nt(a, b):
    return lax.dot_general(a, b, (((1,), (1,)), ((), ())), preferred_element_type=F32)


def _rms_norm(x, g):
    return x * lax.rsqrt(jnp.mean(x * x, axis=-1, keepdims=True) + NORM_EPS) * g


def _split3(x):
    hi = x.astype(BF16)
    r = x - hi.astype(F32)
    mid = r.astype(BF16)
    lo = (r - mid.astype(F32)).astype(BF16)
    return hi, mid, lo


def _pad_rows(x, rows):
    if x.shape[0] == rows:
        return x
    return jnp.concatenate([x, jnp.zeros((rows - x.shape[0], x.shape[1]), x.dtype)], axis=0)


def _chunked_linear_attention(q, k, v, b, bl, chunk, chain, state_ref, o_ref):
    n = q.shape[0]
    nc = n // chunk
    npad = max(n, LANES)
    q_dec = q * jnp.exp(b)
    k_dec = k * jnp.exp(-b)
    k_end = k * jnp.exp(bl - b)
    vb = v.astype(BF16)
    vb_pad = _pad_rows(vb, npad)

    row = lax.broadcasted_iota(jnp.int32, (n, n), 0)
    col = lax.broadcasted_iota(jnp.int32, (n, n), 1)
    causal = ((row // chunk) == (col // chunk)) & (col <= row)
    lane = lax.broadcasted_iota(jnp.int32, (1, LANES), 1)
    half_masks = (lane < DK, lane >= DK)
    lane_t = lax.broadcasted_iota(jnp.int32, (LANES, npad), 1)
    sub = lax.broadcasted_iota(jnp.int32, (LANES, LANES), 0)

    for p in range(HEADS // 2):
        ps = slice(p * LANES, (p + 1) * LANES)
        qp = q_dec[:, ps]
        kpb = k_dec[:, ps].astype(BF16)
        qm = [jnp.where(m, qp, 0.0) for m in half_masks]
        for half in range(2):
            hs = slice((2 * p + half) * DV, (2 * p + half + 1) * DV)
            s = _dot_nt(qm[half].astype(BF16), kpb)
            s = jnp.where(causal, s, 0.0).astype(BF16)
            o_ref[:, hs] = _dot(s, vb[:, hs])

        ke_t = _pad_rows(k_end[:, ps], npad).T
        bl_t = _pad_rows(bl[:, ps], npad).T
        vp = vb_pad[:, 2 * p * DV:(2 * p + 2) * DV]
        if chain:
            s_cur = state_ref[0, p]
        for c in range(nc):
            rs = slice(c * chunk, (c + 1) * chunk)
            if not chain:
                s_cur = state_ref[c, p]
            lhs = jnp.concatenate([qm[0][rs], qm[1][rs]], axis=0).astype(BF16)
            oi = _dot(lhs, s_cur.astype(BF16))
            o_ref[rs, 2 * p * DV:(2 * p + 1) * DV] += oi[:chunk]
            o_ref[rs, (2 * p + 1) * DV:(2 * p + 2) * DV] += oi[chunk:]
            in_chunk = (lane_t >= c * chunk) & (lane_t < (c + 1) * chunk)
            u = _dot(jnp.where(in_chunk, ke_t, 0.0).astype(BF16), vp)
            u_blk = jnp.where(sub < DK, u[:, :DV], u[:, DV:])
            dec = jnp.exp(bl_t[:, c * chunk:c * chunk + 1])
            s_cur = dec * s_cur + u_blk
            if not chain:
                state_ref[c, p] = s_cur
        if chain:
            state_ref[0, p] = s_cur


def _memory_attention_shared(xq, mk_ref, mv_ref, o_ref):
    mkb = mk_ref[0].astype(BF16)
    mvb = mv_ref[0].astype(BF16)
    for h in range(HEADS):
        hs = slice(h * DV, (h + 1) * DV)
        s = _dot_nt(xq[:, hs].astype(BF16), mkb[:, hs]) * (DV ** -0.5)
        e = jnp.exp(s - jnp.max(s, axis=-1, keepdims=True))
        prob = e / jnp.sum(e, axis=-1, keepdims=True)
        o_ref[:, hs] = _dot(prob.astype(BF16), mvb[:, hs])


def _memory_attention_per_seq(xq, rows, mk_ref, mv_ref, o_ref):
    lane = lax.broadcasted_iota(jnp.int32, (1, V_W), 1)
    for c in range(mk_ref.shape[0]):
        qb = xq[c * rows:(c + 1) * rows]
        lhs = jnp.concatenate(
            [jnp.where((lane >= h * DV) & (lane < (h + 1) * DV), qb, 0.0) for h in range(HEADS)], axis=0)
        s = _dot_nt(lhs.astype(BF16), mk_ref[c].astype(BF16)) * (DV ** -0.5)
        e = jnp.exp(s - jnp.max(s, axis=-1, keepdims=True))
        prob = e / jnp.sum(e, axis=-1, keepdims=True)
        pv = _dot(prob.astype(BF16), mv_ref[c].astype(BF16))
        for h in range(HEADS):
            o_ref[c * rows:(c + 1) * rows, h * DV:(h + 1) * DV] = pv[h * rows:(h + 1) * rows, h * DV:(h + 1) * DV]


def _mixer_kernel(x_ref, cos_ref, sin_ref, mk_ref, mv_ref, sg_in_ref, sr_in_ref,
                  n1g_ref, win_ref, aup_ref, ab_ref, gon_ref, rgn_ref, rld_ref,
                  wbg_ref, wbr_ref, wbx_ref, bg_ref, wout_ref,
                  xo_ref, sg_ref, sr_ref, og_ref, or_ref, ox_ref, *, chunk, chain):
    n = x_ref.shape[0]

    @pl.when(pl.program_id(1) == 0)
    def _():
        sg_ref[...] = sg_in_ref[...]
        sr_ref[...] = sr_in_ref[...]

    x = x_ref[...]
    hb = _rms_norm(x, n1g_ref[...]).astype(BF16)

    def proj(off, width):
        return _dot(hb, win_ref[:, off:off + width])

    row = lax.broadcasted_iota(jnp.int32, (n, n), 0)
    col = lax.broadcasted_iota(jnp.int32, (n, n), 1)
    same = (row // chunk) == (col // chunk)
    tri = (same & (col <= row)).astype(BF16)
    blk = same.astype(BF16)

    ga = proj(OFF_GA, LANES).astype(BF16)
    z = _dot(ga, aup_ref[...]) + ab_ref[...]
    lc = (jnp.minimum(z, 0.0) - jnp.log1p(jnp.exp(-jnp.abs(z)))) / GLA_TAU
    parts = _split3(lc)
    b = _dot(tri, parts[0]) + _dot(tri, parts[1]) + _dot(tri, parts[2])
    bl = _dot(blk, parts[0]) + _dot(blk, parts[1]) + _dot(blk, parts[2])
    _chunked_linear_attention(proj(OFF_GQ, QK_W) * (DK ** -0.5), proj(OFF_GK, QK_W), proj(OFF_GV, V_W),
                              b, bl, chunk, chain, sg_ref, og_ref)

    lane = lax.broadcasted_iota(jnp.int32, (1, QK_W), 1)
    first_half = (lane % DK) < (DK // 2)
    cos = cos_ref[...]
    sin = sin_ref[...]

    def rope(t):
        rot = jnp.where(first_half, pltpu.roll(t, QK_W - DK // 2, 1), pltpu.roll(t, DK // 2, 1))
        return t * cos + rot * sin

    t_in_chunk = (lax.broadcasted_iota(jnp.int32, (n, 1), 0) % chunk + 1).astype(F32)
    rld = rld_ref[...]
    _chunked_linear_attention(rope(proj(OFF_RQ, QK_W)), rope(proj(OFF_RK, QK_W)) * (DK ** -0.5),
                              proj(OFF_RV, V_W), t_in_chunk * rld,
                              jnp.broadcast_to(float(chunk) * rld, (n, QK_W)), chunk, chain, sr_ref, or_ref)

    xq = proj(OFF_XQ, V_W)
    if chain:
        _memory_attention_shared(xq, mk_ref, mv_ref, ox_ref)
    else:
        _memory_attention_per_seq(xq, chunk, mk_ref, mv_ref, ox_ref)

    gon = gon_ref[...]
    rgn = rgn_ref[...]
    for h in range(HEADS):
        hs = slice(h * DV, (h + 1) * DV)
        og = og_ref[:, hs]
        og_ref[:, hs] = og * lax.rsqrt(jnp.mean(og * og, axis=-1, keepdims=True) + NORM_EPS) * gon[:, hs]
        orr = or_ref[:, hs]
        oc = orr - jnp.mean(orr, axis=-1, keepdims=True)
        or_ref[:, hs] = oc * lax.rsqrt(jnp.mean(oc * oc, axis=-1, keepdims=True) + GROUP_NORM_EPS) * rgn[:, hs]
    o_gla = (og_ref[...] * jax.nn.silu(proj(OFF_GR, V_W))).astype(BF16)
    o_ret = (or_ref[...] * jax.nn.silu(proj(OFF_RG, V_W))).astype(BF16)
    o_xa = ox_ref[...].astype(BF16)

    def gate(j):
        return jax.nn.sigmoid(proj(OFF_GATES + j * D_MODEL, D_MODEL) + bg_ref[:, j * D_MODEL:(j + 1) * D_MODEL])

    merged = (gate(0) * _dot(o_gla, wbg_ref[...]) + gate(1) * _dot(o_ret, wbr_ref[...])
              + gate(2) * _dot(o_xa, wbx_ref[...]))
    xo_ref[...] = x + _dot(merged.astype(BF16), wout_ref[...])


def _ffn_kernel(x_ref, cb_in_ref, n2g_ref, wup_ref, cw_ref, cbias_ref, wdown_ref, fg_ref,
                xo_ref, cb_ref, *, final_norm):
    n = x_ref.shape[0]
    sb = cb_ref.shape[0]
    rows = n // sb

    @pl.when(pl.program_id(1) == 0)
    def _():
        cb_ref[...] = cb_in_ref[...]

    x = x_ref[...]
    hb = _rms_norm(x, n2g_ref[...]).astype(BF16)
    a = _dot(hb, wup_ref[:, :D_FF])
    gate_in = _dot(hb, wup_ref[:, D_FF:])

    cb = cb_ref[...]
    buf0 = jnp.broadcast_to(cb[:, 0:1, :], (sb, rows, D_FF)).reshape(n, D_FF)
    buf1 = jnp.broadcast_to(cb[:, 1:2, :], (sb, rows, D_FF)).reshape(n, D_FF)
    t = lax.broadcasted_iota(jnp.int32, (n, 1), 0) % rows
    prev1 = jnp.where(t == 0, buf1, pltpu.roll(a, 1, 0))
    prev2 = jnp.where(t == 0, buf0, jnp.where(t == 1, buf1, pltpu.roll(a, 2, 0)))
    cw = cw_ref[...]
    c = cbias_ref[...] + prev2 * cw[0:1] + prev1 * cw[1:2] + a * cw[2:3]
    cb_ref[...] = a.reshape(sb, rows, D_FF)[:, rows - (CONV_W - 1):, :]

    y = x + _dot((jax.nn.gelu(c) * gate_in).astype(BF16), wdown_ref[...])
    if final_norm:
        y = _rms_norm(y, fg_ref[...])
    xo_ref[...] = y


def _memkv_kernel(m_ref, g_ref, w_ref, k_ref, v_ref):
    hb = _rms_norm(m_ref[...], g_ref[0]).astype(BF16)
    k_ref[0] = _dot(hb, w_ref[0, :, :V_W])
    v_ref[0] = _dot(hb, w_ref[0, :, V_W:])


def _resident(shape):
    return pl.BlockSpec(shape, lambda *_: (0,) * len(shape), pipeline_mode=pl.Buffered(1))


def _compiler_params():
    return pltpu.CompilerParams(dimension_semantics=("arbitrary", "arbitrary"),
                                vmem_limit_bytes=VMEM_LIMIT_BYTES)


def _mixer_call(x, cos, sin, mk, mv, sg, sr, w, *, n, sb, nl, chunk, chain, name):
    tokens = x.shape[0]
    nb = tokens // (n * nl)
    x_spec = pl.BlockSpec((n, D_MODEL), lambda i, l: (i * nl + l, 0))
    pos_spec = pl.BlockSpec((n, QK_W), lambda i, l: (l, 0))
    mem_spec = pl.BlockSpec((sb, MEM_LEN, V_W), lambda i, l: (i, 0, 0))
    st_spec = pl.BlockSpec((sb, HEADS // 2, LANES, LANES), lambda i, l: (i, 0, 0, 0))
    weights = (w['norm1_g'], w['w_in'], w['gla_a_up'], w['gla_a_b'], w['gla_onorm_g'], w['ret_gnorm_g'],
               w['ret_log_decay'], w['w_br_gla'], w['w_br_ret'], w['w_br_xa'], w['b_gate'], w['w_out'])
    return pl.pallas_call(
        functools.partial(_mixer_kernel, chunk=chunk, chain=chain),
        grid=(nb, nl),
        in_specs=[x_spec, pos_spec, pos_spec, mem_spec, mem_spec, st_spec, st_spec]
                 + [_resident(a.shape) for a in weights],
        out_specs=[x_spec, st_spec, st_spec],
        out_shape=[jax.ShapeDtypeStruct(x.shape, F32), jax.ShapeDtypeStruct(sg.shape, F32),
                   jax.ShapeDtypeStruct(sr.shape, F32)],
        scratch_shapes=[pltpu.VMEM((n, V_W), F32)] * 3,
        compiler_params=_compiler_params(),
        name=name,
    )(x, cos, sin, mk, mv, sg, sr, *weights)


def _ffn_call(x, cb, w, final_g, *, n, sb, nl, final_norm, name):
    tokens = x.shape[0]
    nb = tokens // (n * nl)
    x_spec = pl.BlockSpec((n, D_MODEL), lambda i, l: (i * nl + l, 0))
    cb_spec = pl.BlockSpec((sb, CONV_W - 1, D_FF), lambda i, l: (i, 0, 0))
    weights = (w['norm2_g'], w['w_up'], w['conv_w'], w['conv_b'], w['w_down'], final_g)
    return pl.pallas_call(
        functools.partial(_ffn_kernel, final_norm=final_norm),
        grid=(nb, nl),
        in_specs=[x_spec, cb_spec] + [_resident(a.shape) for a in weights],
        out_specs=[x_spec, cb_spec],
        out_shape=[jax.ShapeDtypeStruct(x.shape, F32), jax.ShapeDtypeStruct(cb.shape, F32)],
        compiler_params=_compiler_params(),
        name=name,
    )(x, cb, *weights)


def _memkv_call(mem, g, w):
    rows = mem.shape[0]
    tile = 512
    out = jax.ShapeDtypeStruct((DEPTH, rows, V_W), F32)
    kv_spec = pl.BlockSpec((1, tile, V_W), lambda d, r: (d, r, 0))
    return pl.pallas_call(
        _memkv_kernel,
        grid=(DEPTH, rows // tile),
        in_specs=[pl.BlockSpec((tile, D_MODEL), lambda d, r: (r, 0)),
                  pl.BlockSpec((1, 1, D_MODEL), lambda d, r: (d, 0, 0)),
                  pl.BlockSpec((1, D_MODEL, 2 * V_W), lambda d, r: (d, 0, 0))],
        out_specs=[kv_spec, kv_spec],
        out_shape=[out, out],
        compiler_params=_compiler_params(),
        name="memory_kv",
    )(mem, g, w)


def _rope_tables(pos):
    half = DK // 2
    inv = ROPE_BASE ** (-jnp.arange(half, dtype=F32) / half)
    ang = pos.astype(F32)[:, None] * inv[None, :]
    cos = jnp.cos(ang)
    sin = jnp.sin(ang)
    return (jnp.tile(jnp.concatenate([cos, cos], axis=-1), (1, HEADS)),
            jnp.tile(jnp.concatenate([-sin, sin], axis=-1), (1, HEADS)))


def kernel(x_prompt, x_sample, mem_prompt, state_gla, state_ret, state_ffn_conv, cache_mem_k, cache_mem_v, norm1_g, w_in, gla_a_up, gla_a_b, gla_onorm_g, ret_gnorm_g, mem_norm_g, w_mem_kv, w_br_gla, w_br_ret, w_br_xa, b_gate, w_out, norm2_g, w_up, conv_w, conv_b, w_down, final_g):
    bp, lp, _ = x_prompt.shape
    bs, ls, _ = x_sample.shape
    chunk_p = math.gcd(lp, MIX_CHUNK)
    chunk_s = math.gcd(ls, MIX_CHUNK)
    assert lp % PROMPT_TILE == 0 and PROMPT_TILE % chunk_p == 0
    assert chunk_s == ls and bs % SAMPLE_SEQS == 0 and ls >= CONV_W - 1

    w_in_packed = jnp.concatenate(
        [w_in[:, :, :GA_SRC], w_in[:, :, GA_SRC + GLA_LOWRANK:], w_in[:, :, GA_SRC:GA_SRC + GLA_LOWRANK],
         jnp.zeros((DEPTH, D_MODEL, LANES - GLA_LOWRANK), w_in.dtype)], axis=-1).astype(BF16)
    a_up_padded = jnp.concatenate(
        [gla_a_up, jnp.zeros((DEPTH, LANES - GLA_LOWRANK, QK_W), gla_a_up.dtype)], axis=1).astype(BF16)
    ret_log_decay = jnp.repeat(jnp.log(1.0 - 2.0 ** (-5.0 - jnp.arange(HEADS, dtype=F32))), DK)[None, :]
    layers = []
    for i in range(DEPTH):
        layers.append({
            'norm1_g': norm1_g[i][None, :], 'w_in': w_in_packed[i], 'gla_a_up': a_up_padded[i],
            'gla_a_b': gla_a_b[i][None, :], 'gla_onorm_g': gla_onorm_g[i][None, :],
            'ret_gnorm_g': ret_gnorm_g[i][None, :], 'ret_log_decay': ret_log_decay,
            'w_br_gla': w_br_gla[i].astype(BF16), 'w_br_ret': w_br_ret[i].astype(BF16),
            'w_br_xa': w_br_xa[i].astype(BF16), 'b_gate': b_gate[i][None, :], 'w_out': w_out[i].astype(BF16),
            'norm2_g': norm2_g[i][None, :], 'w_up': w_up[i].astype(BF16), 'conv_w': conv_w[i],
            'conv_b': conv_b[i][None, :], 'w_down': w_down[i].astype(BF16)})
    final_g2 = final_g[None, :]

    cos_p, sin_p = _rope_tables(jnp.arange(lp, dtype=jnp.int32))
    cos_s, sin_s = _rope_tables(PAST_LEN + jnp.arange(ls, dtype=jnp.int32))
    cos_s = jnp.tile(cos_s, (SAMPLE_SEQS, 1))
    sin_s = jnp.tile(sin_s, (SAMPLE_SEQS, 1))

    mem_k, mem_v = _memkv_call(mem_prompt.reshape(bp * MEM_LEN, D_MODEL), mem_norm_g[:, None, :],
                               w_mem_kv.astype(BF16))
    mem_k = mem_k.reshape(DEPTH, bp, MEM_LEN, V_W)
    mem_v = mem_v.reshape(DEPTH, bp, MEM_LEN, V_W)

    pair_shape = (HEADS // 2, LANES, LANES)
    zero_state = jnp.zeros((bp,) + pair_shape, F32)
    zero_conv = jnp.zeros((bp, CONV_W - 1, D_FF), F32)
    xp = x_prompt.reshape(bp * lp, D_MODEL)
    xs = x_sample.reshape(bs * ls, D_MODEL)
    p_gla, p_ret, p_conv, s_gla, s_ret, s_conv = [], [], [], [], [], []
    for i in range(DEPTH):
        w = layers[i]
        last = i == DEPTH - 1
        xp, g_new, r_new = _mixer_call(xp, cos_p, sin_p, mem_k[i], mem_v[i], zero_state, zero_state, w,
                                       n=PROMPT_TILE, sb=1, nl=lp // PROMPT_TILE, chunk=chunk_p, chain=True,
                                       name=f"mixer_prompt_{i}")
        xp, c_new = _ffn_call(xp, zero_conv, w, final_g2, n=PROMPT_TILE, sb=1, nl=lp // PROMPT_TILE,
                              final_norm=last, name=f"ffn_prompt_{i}")
        p_gla.append(g_new); p_ret.append(r_new); p_conv.append(c_new)
        xs, g_new, r_new = _mixer_call(xs, cos_s, sin_s,
                                       cache_mem_k[i].reshape(bs, MEM_LEN, V_W),
                                       cache_mem_v[i].reshape(bs, MEM_LEN, V_W),
                                       state_gla[i].reshape((bs,) + pair_shape),
                                       state_ret[i].reshape((bs,) + pair_shape), w,
                                       n=SAMPLE_SEQS * ls, sb=SAMPLE_SEQS, nl=1, chunk=chunk_s, chain=False,
                                       name=f"mixer_sample_{i}")
        xs, c_new = _ffn_call(xs, state_ffn_conv[i], w, final_g2, n=SAMPLE_SEQS * ls, sb=SAMPLE_SEQS, nl=1,
                              final_norm=last, name=f"ffn_sample_{i}")
        s_gla.append(g_new); s_ret.append(r_new); s_conv.append(c_new)

    state_shape = (DEPTH, -1, HEADS, DK, DV)
    kv_shape = (DEPTH, bp, MEM_LEN, HEADS, DV)
    return (xp.reshape(bp, lp, D_MODEL), xs.reshape(bs, ls, D_MODEL),
            jnp.stack(p_gla).reshape(state_shape), jnp.stack(p_ret).reshape(state_shape), jnp.stack(p_conv),
            mem_k.reshape(kv_shape), mem_v.reshape(kv_shape),
            jnp.stack(s_gla).reshape(state_shape), jnp.stack(s_ret).reshape(state_shape), jnp.stack(s_conv))
```

```python
import functools
import math

import jax
import jax.numpy as jnp
from jax import lax
from jax.experimental import pallas as pl
from jax.experimental.pallas import tpu as pltpu

F32 = jnp.float32
BF16 = jnp.bfloat16

D_MODEL = 1024
DEPTH = 2
PAST_LEN = 16384
HEADS = 4
DK = 64
DV = 128
QK_W = HEADS * DK
V_W = HEADS * DV
GLA_LOWRANK = 16
GLA_TAU = 16.0
ROPE_BASE = 10000.0
MIX_CHUNK = 32
MEM_LEN = 256
D_FF = 2816
CONV_W = 3
NORM_EPS = 1e-6
GROUP_NORM_EPS = 1e-5
N_BRANCH = 3

LANES = 128
VMEM_LIMIT_BYTES = 56 * 1024 * 1024

A_GQ = 0
A_GK = A_GQ + QK_W
A_GV = A_GK + QK_W
A_GR = A_GV + V_W
A_COLS = A_GR + V_W
B_RQ = 0
B_RK = B_RQ + QK_W
B_RV = B_RK + QK_W
B_RG = B_RV + V_W
B_XQ = B_RG + V_W
B_GATES = B_XQ + V_W
B_COLS = B_GATES + N_BRANCH * D_MODEL

PROMPT_TILE = 256
PROMPT_FFN_TILE = 512
SAMPLE_SEQS = 8


def _dot(a, b):
    return jnp.dot(a, b, preferred_element_type=F32)


def _dot_nt(a, b):
    return lax.dot_general(a, b, (((1,), (1,)), ((), ())), preferred_element_type=F32)


def _rms_norm(x, g):
    return x * lax.rsqrt(jnp.mean(x * x, axis=-1, keepdims=True) + NORM_EPS) * g


def _split3(x):
    hi = x.astype(BF16)
    r = x - hi.astype(F32)
    mid = r.astype(BF16)
    lo = (r - mid.astype(F32)).astype(BF16)
    return hi, mid, lo


def _pad_rows(x, rows):
    if x.shape[0] == rows:
        return x
    return jnp.concatenate([x, jnp.zeros((rows - x.shape[0], x.shape[1]), x.dtype)], axis=0)


def _chunk_total(b, chunk):
    n, w = b.shape
    last = b.reshape(n // chunk, chunk, w)[:, chunk - 1:chunk, :]
    return jnp.broadcast_to(last, (n // chunk, chunk, w)).reshape(n, w)


def _half_masks():
    lane = lax.broadcasted_iota(jnp.int32, (1, LANES), 1)
    return (lane < DK, lane >= DK)


def _pair_blocks(u):
    sub = lax.broadcasted_iota(jnp.int32, (LANES, DV), 0)
    return jnp.where(sub < DK, u[:, :DV], u[:, DV:])


def _chunked_linear_attention(q, k, v, b, chunk, chain, state_ref):
    n = q.shape[0]
    nc = n // chunk
    npad = max(n, LANES)
    bl = _chunk_total(b, chunk)
    q_dec = q * jnp.exp(b)
    k_dec = k * jnp.exp(-b)
    k_end = k * jnp.exp(bl - b)
    vb = v.astype(BF16)
    vb_pad = _pad_rows(vb, npad)

    row = lax.broadcasted_iota(jnp.int32, (n, n), 0)
    col = lax.broadcasted_iota(jnp.int32, (n, n), 1)
    causal = ((row // chunk) == (col // chunk)) & (col <= row)
    half_masks = _half_masks()
    lane_t = lax.broadcasted_iota(jnp.int32, (LANES, npad), 1)

    outs = []
    for p in range(HEADS // 2):
        ps = slice(p * LANES, (p + 1) * LANES)
        qp = q_dec[:, ps]
        kpb = k_dec[:, ps].astype(BF16)
        qm = [jnp.where(m, qp, 0.0) for m in half_masks]
        intra = []
        for half in range(2):
            hs = slice((2 * p + half) * DV, (2 * p + half + 1) * DV)
            s = _dot_nt(qm[half].astype(BF16), kpb)
            intra.append(_dot(jnp.where(causal, s, 0.0).astype(BF16), vb[:, hs]))

        ke_t = _pad_rows(k_end[:, ps], npad).T
        bl_t = _pad_rows(bl[:, ps], npad).T
        vp = vb_pad[:, 2 * p * DV:(2 * p + 2) * DV]
        inter = ([], [])
        if chain:
            s_cur = state_ref[0, p]
        for c in range(nc):
            rs = slice(c * chunk, (c + 1) * chunk)
            if not chain:
                s_cur = state_ref[c, p]
            lhs = jnp.concatenate([qm[0][rs], qm[1][rs]], axis=0).astype(BF16)
            oi = _dot(lhs, s_cur.astype(BF16))
            inter[0].append(oi[:chunk])
            inter[1].append(oi[chunk:])
            in_chunk = (lane_t >= c * chunk) & (lane_t < (c + 1) * chunk)
            u = _dot(jnp.where(in_chunk, ke_t, 0.0).astype(BF16), vp)
            dec = jnp.exp(bl_t[:, c * chunk:c * chunk + 1])
            s_cur = dec * s_cur + _pair_blocks(u)
            if not chain:
                state_ref[c, p] = s_cur
        if chain:
            state_ref[0, p] = s_cur
        outs.append(intra[0] + jnp.concatenate(inter[0], axis=0))
        outs.append(intra[1] + jnp.concatenate(inter[1], axis=0))
    return outs


def _retention_tile(q, k, v, rld, state_ref):
    n = q.shape[0]
    t = lax.broadcasted_iota(jnp.int32, (n, 1), 0).astype(F32)
    q_dec = q * jnp.exp((t + 1.0) * rld)
    k_end = k * jnp.exp((float(n - 1) - t) * rld)
    vb = v.astype(BF16)
    row = lax.broadcasted_iota(jnp.int32, (n, n), 0)
    col = lax.broadcasted_iota(jnp.int32, (n, n), 1)
    causal = col <= row
    delta = (row - col).astype(F32)
    half_masks = _half_masks()
    sub = lax.broadcasted_iota(jnp.int32, (LANES, 1), 0)

    outs = []
    for p in range(HEADS // 2):
        ps = slice(p * LANES, (p + 1) * LANES)
        kpb = k[:, ps].astype(BF16)
        s_cur = state_ref[0, p]
        s_bf = s_cur.astype(BF16)
        log_g = [rld[:, (2 * p + half) * DK:(2 * p + half) * DK + 1] for half in range(2)]
        for half in range(2):
            hs = slice((2 * p + half) * DV, (2 * p + half + 1) * DV)
            decay = jnp.where(causal, jnp.exp(delta * log_g[half]), 0.0)
            s = _dot_nt(jnp.where(half_masks[half], q[:, ps], 0.0).astype(BF16), kpb) * decay
            outs.append(_dot(s.astype(BF16), vb[:, hs])
                        + _dot(jnp.where(half_masks[half], q_dec[:, ps], 0.0).astype(BF16), s_bf))
        u = _dot(k_end[:, ps].T.astype(BF16), vb[:, 2 * p * DV:(2 * p + 2) * DV])
        dec = jnp.where(sub < DK, jnp.exp(float(n) * log_g[0]), jnp.exp(float(n) * log_g[1]))
        state_ref[0, p] = dec * s_cur + _pair_blocks(u)
    return outs


def _softmax_rows(s):
    e = jnp.exp(s - jnp.max(s, axis=-1, keepdims=True))
    return e / jnp.sum(e, axis=-1, keepdims=True)


def _memory_attention_shared(xq, mk_ref, mv_ref):
    mkb = mk_ref[0].astype(BF16)
    mvb = mv_ref[0].astype(BF16)
    outs = []
    for h in range(HEADS):
        hs = slice(h * DV, (h + 1) * DV)
        s = _dot_nt(xq[:, hs].astype(BF16), mkb[:, hs]) * (DV ** -0.5)
        outs.append(_dot(_softmax_rows(s).astype(BF16), mvb[:, hs]))
    return outs


def _memory_attention_per_seq(xq, rows, mk_ref, mv_ref):
    nseq = mk_ref.shape[0]
    r = lax.broadcasted_iota(jnp.int32, (HEADS * rows, MEM_LEN * HEADS), 0)
    c = lax.broadcasted_iota(jnp.int32, (HEADS * rows, MEM_LEN * HEADS), 1)
    own_head = (c % HEADS) == (r // rows)
    pieces = [[] for _ in range(HEADS)]
    for i in range(nseq):
        qb = xq[i * rows:(i + 1) * rows]
        lhs = jnp.concatenate([qb[:, h * DV:(h + 1) * DV] for h in range(HEADS)], axis=0)
        s = _dot_nt(lhs.astype(BF16), mk_ref[i].astype(BF16)) * (DV ** -0.5)
        prob = _softmax_rows(jnp.where(own_head, s, -1e30))
        pv = _dot(prob.astype(BF16), mv_ref[i].astype(BF16))
        for h in range(HEADS):
            pieces[h].append(pv[h * rows:(h + 1) * rows])
    return [jnp.concatenate(ph, axis=0) for ph in pieces]


def _mixer_kernel(x_ref, cos_ref, sin_ref, mk_ref, mv_ref, sg_in_ref, sr_in_ref, sg_buf_ref, sr_buf_ref,
                  n1g_ref, wa_ref, wb_ref, wga_ref, aup_ref, ab_ref, gon_ref, rgn_ref, rld_ref,
                  wbg_ref, wbr_ref, wbx_ref, bg_ref, wout_ref,
                  xo_ref, sg_ref, sr_ref, *, chunk, chain):
    del sg_buf_ref, sr_buf_ref
    n = x_ref.shape[0]

    @pl.when(pl.program_id(1) == 0)
    def _():
        sg_ref[...] = sg_in_ref[...]
        sr_ref[...] = sr_in_ref[...]

    x = x_ref[...]
    hb = _rms_norm(x, n1g_ref[...]).astype(BF16)

    def proj_a(off, width):
        return _dot(hb, wa_ref[:, off:off + width])

    def proj_b(off, width):
        return _dot(hb, wb_ref[:, off:off + width])

    row = lax.broadcasted_iota(jnp.int32, (n, n), 0)
    col = lax.broadcasted_iota(jnp.int32, (n, n), 1)
    tri = (((row // chunk) == (col // chunk)) & (col <= row)).astype(BF16)
    ga = _dot(hb, wga_ref[...]).astype(BF16)
    z = _dot(ga, aup_ref[...]) + ab_ref[...]
    lc = (jnp.minimum(z, 0.0) - jnp.log1p(jnp.exp(-jnp.abs(z)))) / GLA_TAU
    parts = _split3(lc)
    b = _dot(tri, parts[0]) + _dot(tri, parts[1]) + _dot(tri, parts[2])
    o_gla = _chunked_linear_attention(proj_a(A_GQ, QK_W) * (DK ** -0.5), proj_a(A_GK, QK_W), proj_a(A_GV, V_W),
                                      b, chunk, chain, sg_ref)

    lane = lax.broadcasted_iota(jnp.int32, (1, QK_W), 1)
    first_half = (lane % DK) < (DK // 2)
    cos = cos_ref[...]
    sin = sin_ref[...]

    def rope(t):
        rot = jnp.where(first_half, pltpu.roll(t, QK_W - DK // 2, 1), pltpu.roll(t, DK // 2, 1))
        return t * cos + rot * sin

    rq = rope(proj_b(B_RQ, QK_W))
    rk = rope(proj_b(B_RK, QK_W)) * (DK ** -0.5)
    rv = proj_b(B_RV, V_W)
    rld = rld_ref[...]
    if chain:
        o_ret = _retention_tile(rq, rk, rv, rld, sr_ref)
    else:
        t_in_chunk = (lax.broadcasted_iota(jnp.int32, (n, 1), 0) % chunk + 1).astype(F32)
        o_ret = _chunked_linear_attention(rq, rk, rv, t_in_chunk * rld, chunk, chain, sr_ref)

    xq = proj_b(B_XQ, V_W)
    if chain:
        o_xa = _memory_attention_shared(xq, mk_ref, mv_ref)
    else:
        o_xa = _memory_attention_per_seq(xq, chunk, mk_ref, mv_ref)

    gon = gon_ref[...]
    rgn = rgn_ref[...]
    for h in range(HEADS):
        hs = slice(h * DV, (h + 1) * DV)
        og = o_gla[h]
        o_gla[h] = og * lax.rsqrt(jnp.mean(og * og, axis=-1, keepdims=True) + NORM_EPS) * gon[:, hs]
        oc = o_ret[h] - jnp.mean(o_ret[h], axis=-1, keepdims=True)
        o_ret[h] = oc * lax.rsqrt(jnp.mean(oc * oc, axis=-1, keepdims=True) + GROUP_NORM_EPS) * rgn[:, hs]
    o_gla = (jnp.concatenate(o_gla, axis=1) * jax.nn.silu(proj_a(A_GR, V_W))).astype(BF16)
    o_ret = (jnp.concatenate(o_ret, axis=1) * jax.nn.silu(proj_b(B_RG, V_W))).astype(BF16)
    o_xa = jnp.concatenate(o_xa, axis=1).astype(BF16)

    def gate(j):
        return jax.nn.sigmoid(proj_b(B_GATES + j * D_MODEL, D_MODEL) + bg_ref[:, j * D_MODEL:(j + 1) * D_MODEL])

    merged = (gate(0) * _dot(o_gla, wbg_ref[...]) + gate(1) * _dot(o_ret, wbr_ref[...])
              + gate(2) * _dot(o_xa, wbx_ref[...]))
    xo_ref[...] = x + _dot(merged.astype(BF16), wout_ref[...])


def _ffn_kernel(x_ref, cb_in_ref, cb_buf_ref, n2g_ref, wup_ref, cw_ref, cbias_ref, wdown_ref, fg_ref,
                xo_ref, cb_ref, *, final_norm):
    del cb_buf_ref
    n = x_ref.shape[0]
    sb = cb_ref.shape[0]
    rows = n // sb

    @pl.when(pl.program_id(1) == 0)
    def _():
        cb_ref[...] = cb_in_ref[...]

    x = x_ref[...]
    hb = _rms_norm(x, n2g_ref[...]).astype(BF16)
    a = _dot(hb, wup_ref[:, :D_FF])
    gate_in = _dot(hb, wup_ref[:, D_FF:])

    cb = cb_ref[...]
    buf0 = jnp.broadcast_to(cb[:, 0:1, :], (sb, rows, D_FF)).reshape(n, D_FF)
    buf1 = jnp.broadcast_to(cb[:, 1:2, :], (sb, rows, D_FF)).reshape(n, D_FF)
    t = lax.broadcasted_iota(jnp.int32, (n, 1), 0) % rows
    prev1 = jnp.where(t == 0, buf1, pltpu.roll(a, 1, 0))
    prev2 = jnp.where(t == 0, buf0, jnp.where(t == 1, buf1, pltpu.roll(a, 2, 0)))
    cw = cw_ref[...]
    c = cbias_ref[...] + prev2 * cw[0:1] + prev1 * cw[1:2] + a * cw[2:3]
    cb_ref[...] = a.reshape(sb, rows, D_FF)[:, rows - (CONV_W - 1):, :]

    y = x + _dot((jax.nn.gelu(c) * gate_in).astype(BF16), wdown_ref[...])
    if final_norm:
        y = _rms_norm(y, fg_ref[...])
    xo_ref[...] = y


def _memkv_kernel(m_ref, g_ref, w_ref, k_ref, v_ref):
    hb = _rms_norm(m_ref[...], g_ref[...]).astype(BF16)
    k_ref[...] = _dot(hb, w_ref[:, :V_W])
    v_ref[...] = _dot(hb, w_ref[:, V_W:])


def _layer_resident(a, layer):
    zeros = (0,) * (a.ndim - 1)
    return pl.BlockSpec((None,) + a.shape[1:], lambda *_: (layer,) + zeros, pipeline_mode=pl.Buffered(1))


def _layer_blocks(a, layer, sb):
    zeros = (0,) * (a.ndim - 2)
    return pl.BlockSpec((None, sb) + a.shape[2:], lambda i, l: (layer, i) + zeros)


def _compiler_params():
    return pltpu.CompilerParams(dimension_semantics=("arbitrary", "arbitrary"),
                                vmem_limit_bytes=VMEM_LIMIT_BYTES)


_ANY = pl.BlockSpec(memory_space=pl.ANY)


def _mixer_call(x, cos, sin, mk, mv, sg_in, sr_in, sg_buf, sr_buf, w, *, layer, n, sb, nl, chunk, chain, name):
    tokens = x.shape[0]
    nb = tokens // (n * nl)
    x_spec = pl.BlockSpec((n, D_MODEL), lambda i, l: (i * nl + l, 0))
    pos_spec = pl.BlockSpec((n, QK_W), lambda i, l: (l, 0))
    weights = (w['norm1_g'], w['w_in_a'], w['w_in_b'], w['w_in_ga'], w['gla_a_up'], w['gla_a_b'],
               w['gla_onorm_g'], w['ret_gnorm_g'], w['ret_log_decay'], w['w_br_gla'], w['w_br_ret'],
               w['w_br_xa'], w['b_gate'], w['w_out'])
    return pl.pallas_call(
        functools.partial(_mixer_kernel, chunk=chunk, chain=chain),
        grid=(nb, nl),
        in_specs=[x_spec, pos_spec, pos_spec, _layer_blocks(mk, layer, sb), _layer_blocks(mv, layer, sb),
                  _layer_blocks(sg_in, layer, sb), _layer_blocks(sr_in, layer, sb), _ANY, _ANY]
                 + [_layer_resident(a, layer) for a in weights],
        out_specs=[x_spec, _layer_blocks(sg_buf, layer, sb), _layer_blocks(sr_buf, layer, sb)],
        out_shape=[jax.ShapeDtypeStruct(x.shape, F32), jax.ShapeDtypeStruct(sg_buf.shape, F32),
                   jax.ShapeDtypeStruct(sr_buf.shape, F32)],
        input_output_aliases={7: 1, 8: 2},
        compiler_params=_compiler_params(),
        name=name,
    )(x, cos, sin, mk, mv, sg_in, sr_in, sg_buf, sr_buf, *weights)


def _ffn_call(x, cb_in, cb_buf, w, *, layer, n, sb, nl, final_norm, name):
    tokens = x.shape[0]
    nb = tokens // (n * nl)
    x_spec = pl.BlockSpec((n, D_MODEL), lambda i, l: (i * nl + l, 0))
    weights = (w['norm2_g'], w['w_up'], w['conv_w'], w['conv_b'], w['w_down'], w['final_g'])
    return pl.pallas_call(
        functools.partial(_ffn_kernel, final_norm=final_norm),
        grid=(nb, nl),
        in_specs=[x_spec, _layer_blocks(cb_in, layer, sb), _ANY] + [_layer_resident(a, layer) for a in weights],
        out_specs=[x_spec, _layer_blocks(cb_buf, layer, sb)],
        out_shape=[jax.ShapeDtypeStruct(x.shape, F32), jax.ShapeDtypeStruct(cb_buf.shape, F32)],
        input_output_aliases={2: 1},
        compiler_params=_compiler_params(),
        name=name,
    )(x, cb_in, cb_buf, *weights)


def _memkv_call(mem, g, w):
    rows = mem.shape[0]
    tile = 512
    out = jax.ShapeDtypeStruct((DEPTH, rows, V_W), F32)
    kv_spec = pl.BlockSpec((None, tile, V_W), lambda d, r: (d, r, 0))
    return pl.pallas_call(
        _memkv_kernel,
        grid=(DEPTH, rows // tile),
        in_specs=[pl.BlockSpec((tile, D_MODEL), lambda d, r: (r, 0)),
                  pl.BlockSpec((None, 1, D_MODEL), lambda d, r: (d, 0, 0)),
                  pl.BlockSpec((None, D_MODEL, 2 * V_W), lambda d, r: (d, 0, 0))],
        out_specs=[kv_spec, kv_spec],
        out_shape=[out, out],
        compiler_params=_compiler_params(),
        name="memory_kv",
    )(mem, g, w)


def _rope_tables(pos):
    half = DK // 2
    inv = ROPE_BASE ** (-jnp.arange(half, dtype=F32) / half)
    ang = pos.astype(F32)[:, None] * inv[None, :]
    cos = jnp.cos(ang)
    sin = jnp.sin(ang)
    return (jnp.tile(jnp.concatenate([cos, cos], axis=-1), (1, HEADS)),
            jnp.tile(jnp.concatenate([-sin, sin], axis=-1), (1, HEADS)))


def kernel(x_prompt, x_sample, mem_prompt, state_gla, state_ret, state_ffn_conv, cache_mem_k, cache_mem_v, norm1_g, w_in, gla_a_up, gla_a_b, gla_onorm_g, ret_gnorm_g, mem_norm_g, w_mem_kv, w_br_gla, w_br_ret, w_br_xa, b_gate, w_out, norm2_g, w_up, conv_w, conv_b, w_down, final_g):
    bp, lp, _ = x_prompt.shape
    bs, ls, _ = x_sample.shape
    chunk_p = math.gcd(lp, MIX_CHUNK)
    chunk_s = math.gcd(ls, MIX_CHUNK)
    assert lp % PROMPT_TILE == 0 and PROMPT_TILE % chunk_p == 0 and lp % PROMPT_FFN_TILE == 0
    assert chunk_s == ls and bs % SAMPLE_SEQS == 0 and ls >= CONV_W - 1

    ga_src = A_COLS
    w = {
        'norm1_g': norm1_g[:, None, :],
        'w_in_a': w_in[:, :, :ga_src].astype(BF16),
        'w_in_b': w_in[:, :, ga_src + GLA_LOWRANK:].astype(BF16),
        'w_in_ga': jnp.pad(w_in[:, :, ga_src:ga_src + GLA_LOWRANK].astype(BF16),
                           ((0, 0), (0, 0), (0, LANES - GLA_LOWRANK))),
        'gla_a_up': jnp.pad(gla_a_up.astype(BF16), ((0, 0), (0, LANES - GLA_LOWRANK), (0, 0))),
        'gla_a_b': gla_a_b[:, None, :], 'gla_onorm_g': gla_onorm_g[:, None, :],
        'ret_gnorm_g': ret_gnorm_g[:, None, :],
        'ret_log_decay': jnp.broadcast_to(
            jnp.repeat(jnp.log(1.0 - 2.0 ** (-5.0 - jnp.arange(HEADS, dtype=F32))), DK)[None, None, :],
            (DEPTH, 1, QK_W)),
        'w_br_gla': w_br_gla.astype(BF16), 'w_br_ret': w_br_ret.astype(BF16), 'w_br_xa': w_br_xa.astype(BF16),
        'b_gate': b_gate[:, None, :], 'w_out': w_out.astype(BF16),
        'norm2_g': norm2_g[:, None, :], 'w_up': w_up.astype(BF16), 'conv_w': conv_w,
        'conv_b': conv_b[:, None, :], 'w_down': w_down.astype(BF16),
        'final_g': jnp.broadcast_to(final_g[None, None, :], (DEPTH, 1, D_MODEL)),
    }

    cos_p, sin_p = _rope_tables(jnp.arange(lp, dtype=jnp.int32))
    cos_s, sin_s = _rope_tables(PAST_LEN + jnp.arange(ls, dtype=jnp.int32))
    cos_s = jnp.tile(cos_s, (SAMPLE_SEQS, 1))
    sin_s = jnp.tile(sin_s, (SAMPLE_SEQS, 1))

    mem_k, mem_v = _memkv_call(mem_prompt.reshape(bp * MEM_LEN, D_MODEL), mem_norm_g[:, None, :],
                               w_mem_kv.astype(BF16))
    mem_k = mem_k.reshape(DEPTH, bp, MEM_LEN, V_W)
    mem_v = mem_v.reshape(DEPTH, bp, MEM_LEN, V_W)
    cache_k = cache_mem_k.reshape(DEPTH, bs, MEM_LEN * HEADS, DV)
    cache_v = cache_mem_v.reshape(DEPTH, bs, MEM_LEN * HEADS, DV)

    pair_shape = (HEADS // 2, LANES, LANES)
    p_zero_state = jnp.zeros((DEPTH, bp) + pair_shape, F32)
    p_zero_conv = jnp.zeros((DEPTH, bp, CONV_W - 1, D_FF), F32)
    s_gla_in = state_gla.reshape((DEPTH, bs) + pair_shape)
    s_ret_in = state_ret.reshape((DEPTH, bs) + pair_shape)
    p_gla, p_ret, p_conv = p_zero_state, p_zero_state, p_zero_conv
    s_gla = jnp.zeros_like(s_gla_in)
    s_ret = jnp.zeros_like(s_ret_in)
    s_conv = jnp.zeros_like(state_ffn_conv)

    xp = x_prompt.reshape(bp * lp, D_MODEL)
    xs = x_sample.reshape(bs * ls, D_MODEL)
    for i in range(DEPTH):
        last = i == DEPTH - 1
        xp, p_gla, p_ret = _mixer_call(xp, cos_p, sin_p, mem_k, mem_v, p_zero_state, p_zero_state, p_gla, p_ret, w,
                                       layer=i, n=PROMPT_TILE, sb=1, nl=lp // PROMPT_TILE, chunk=chunk_p,
                                       chain=True, name=f"mixer_prompt_{i}")
        xp, p_conv = _ffn_call(xp, p_zero_conv, p_conv, w, layer=i, n=PROMPT_FFN_TILE, sb=1,
                               nl=lp // PROMPT_FFN_TILE,
                               final_norm=last, name=f"ffn_prompt_{i}")
        xs, s_gla, s_ret = _mixer_call(xs, cos_s, sin_s, cache_k, cache_v, s_gla_in, s_ret_in, s_gla, s_ret, w,
                                       layer=i, n=SAMPLE_SEQS * ls, sb=SAMPLE_SEQS, nl=1, chunk=chunk_s,
                                       chain=False, name=f"mixer_sample_{i}")
        xs, s_conv = _ffn_call(xs, state_ffn_conv, s_conv, w, layer=i, n=SAMPLE_SEQS * ls, sb=SAMPLE_SEQS, nl=1,
                               final_norm=last, name=f"ffn_sample_{i}")

    kv_shape = (DEPTH, bp, MEM_LEN, HEADS, DV)
    return (xp.reshape(bp, lp, D_MODEL), xs.reshape(bs, ls, D_MODEL),
            p_gla.reshape(DEPTH, bp, HEADS, DK, DV), p_ret.reshape(DEPTH, bp, HEADS, DK, DV), p_conv,
            mem_k.reshape(kv_shape), mem_v.reshape(kv_shape),
            s_gla.reshape(DEPTH, bs, HEADS, DK, DV), s_ret.reshape(DEPTH, bs, HEADS, DK, DV), s_conv)
```

```python
import functools
import math

import jax
import jax.numpy as jnp
from jax import lax
from jax.experimental import pallas as pl
from jax.experimental.pallas import tpu as pltpu

F32 = jnp.float32
BF16 = jnp.bfloat16

D_MODEL = 1024
DEPTH = 2
PAST_LEN = 16384
HEADS = 4
DK = 64
DV = 128
QK_W = HEADS * DK
V_W = HEADS * DV
GLA_LOWRANK = 16
GLA_TAU = 16.0
ROPE_BASE = 10000.0
MIX_CHUNK = 32
MEM_LEN = 256
D_FF = 2816
CONV_W = 3
NORM_EPS = 1e-6
GROUP_NORM_EPS = 1e-5
N_BRANCH = 3

LANES = 128
VMEM_LIMIT_BYTES = 56 * 1024 * 1024

A_GQ = 0
A_GK = A_GQ + QK_W
A_GV = A_GK + QK_W
A_GR = A_GV + V_W
A_COLS = A_GR + V_W
B_RQ = 0
B_RK = B_RQ + QK_W
B_RV = B_RK + QK_W
B_RG = B_RV + V_W
B_XQ = B_RG + V_W
B_GATES = B_XQ + V_W
B_COLS = B_GATES + N_BRANCH * D_MODEL

PROMPT_TILE = 256
PROMPT_FFN_TILE = 512
SAMPLE_SEQS = 8
SAMPLE_FFN_SEQS = 32


def _dot(a, b):
    return jnp.dot(a, b, preferred_element_type=F32)


def _dot_nt(a, b):
    return lax.dot_general(a, b, (((1,), (1,)), ((), ())), preferred_element_type=F32)


def _rms_norm(x, g):
    return x * lax.rsqrt(jnp.mean(x * x, axis=-1, keepdims=True) + NORM_EPS) * g


def _split3(x):
    hi = x.astype(BF16)
    r = x - hi.astype(F32)
    mid = r.astype(BF16)
    lo = (r - mid.astype(F32)).astype(BF16)
    return hi, mid, lo


def _pad_rows(x, rows):
    if x.shape[0] == rows:
        return x
    return jnp.concatenate([x, jnp.zeros((rows - x.shape[0], x.shape[1]), x.dtype)], axis=0)


def _chunk_total(b, chunk):
    n, w = b.shape
    last = b.reshape(n // chunk, chunk, w)[:, chunk - 1:chunk, :]
    return jnp.broadcast_to(last, (n // chunk, chunk, w)).reshape(n, w)


def _half_masks():
    lane = lax.broadcasted_iota(jnp.int32, (1, LANES), 1)
    return (lane < DK, lane >= DK)


def _pair_blocks(u):
    sub = lax.broadcasted_iota(jnp.int32, (LANES, DV), 0)
    return jnp.where(sub < DK, u[:, :DV], u[:, DV:])


def _chunked_linear_attention(q, k, v, b, chunk, chain, state_ref):
    n = q.shape[0]
    nc = n // chunk
    npad = max(n, LANES)
    bl = _chunk_total(b, chunk)
    q_dec = q * jnp.exp(b)
    k_dec = k * jnp.exp(-b)
    k_end = k * jnp.exp(bl - b)
    vb = v.astype(BF16)
    vb_pad = _pad_rows(vb, npad)

    row = lax.broadcasted_iota(jnp.int32, (n, n), 0)
    col = lax.broadcasted_iota(jnp.int32, (n, n), 1)
    causal = ((row // chunk) == (col // chunk)) & (col <= row)
    half_masks = _half_masks()
    lane_t = lax.broadcasted_iota(jnp.int32, (LANES, npad), 1)

    pairs = range(HEADS // 2)
    chunks = range(nc)
    qm, intra, u_blk, dec = [], [], [], []
    for p in pairs:
        ps = slice(p * LANES, (p + 1) * LANES)
        qp = q_dec[:, ps]
        kpb = k_dec[:, ps].astype(BF16)
        qm.append([jnp.where(m, qp, 0.0) for m in half_masks])
        for half in range(2):
            hs = slice((2 * p + half) * DV, (2 * p + half + 1) * DV)
            s = _dot_nt(qm[p][half].astype(BF16), kpb)
            intra.append(_dot(jnp.where(causal, s, 0.0).astype(BF16), vb[:, hs]))
        ke_t = _pad_rows(k_end[:, ps], npad).T
        bl_t = _pad_rows(bl[:, ps], npad).T
        vp = vb_pad[:, 2 * p * DV:(2 * p + 2) * DV]
        u_blk.append([])
        dec.append([])
        for c in chunks:
            in_chunk = (lane_t >= c * chunk) & (lane_t < (c + 1) * chunk)
            u_blk[p].append(_pair_blocks(_dot(jnp.where(in_chunk, ke_t, 0.0).astype(BF16), vp)))
            dec[p].append(jnp.exp(bl_t[:, c * chunk:c * chunk + 1]))

    s_start = []
    for p in pairs:
        s_start.append([])
        if chain:
            s_cur = state_ref[0, p]
        for c in chunks:
            if not chain:
                s_cur = state_ref[c, p]
            s_start[p].append(s_cur.astype(BF16))
            s_cur = dec[p][c] * s_cur + u_blk[p][c]
            if not chain:
                state_ref[c, p] = s_cur
        if chain:
            state_ref[0, p] = s_cur

    outs = []
    for p in pairs:
        inter = ([], [])
        for c in chunks:
            rs = slice(c * chunk, (c + 1) * chunk)
            lhs = jnp.concatenate([qm[p][0][rs], qm[p][1][rs]], axis=0).astype(BF16)
            oi = _dot(lhs, s_start[p][c])
            inter[0].append(oi[:chunk])
            inter[1].append(oi[chunk:])
        outs.append(intra[2 * p] + jnp.concatenate(inter[0], axis=0))
        outs.append(intra[2 * p + 1] + jnp.concatenate(inter[1], axis=0))
    return outs


def _retention_tile(q, k, v, rld, state_ref):
    n = q.shape[0]
    t = lax.broadcasted_iota(jnp.int32, (n, 1), 0).astype(F32)
    q_dec = q * jnp.exp((t + 1.0) * rld)
    k_end = k * jnp.exp((float(n - 1) - t) * rld)
    vb = v.astype(BF16)
    row = lax.broadcasted_iota(jnp.int32, (n, n), 0)
    col = lax.broadcasted_iota(jnp.int32, (n, n), 1)
    causal = col <= row
    delta = (row - col).astype(F32)
    half_masks = _half_masks()
    sub = lax.broadcasted_iota(jnp.int32, (LANES, 1), 0)

    pair_lanes = [slice(p * LANES, (p + 1) * LANES) for p in range(HEADS // 2)]
    log_g = [rld[:, h * DK:h * DK + 1] for h in range(HEADS)]
    raw = [_dot_nt(jnp.where(half_masks[h % 2], q[:, pair_lanes[h // 2]], 0.0).astype(BF16),
                   k[:, pair_lanes[h // 2]].astype(BF16)) for h in range(HEADS)]
    scores = [(raw[h] * jnp.where(causal, jnp.exp(delta * log_g[h]), 0.0)).astype(BF16) for h in range(HEADS)]
    s_cur = [state_ref[0, p] for p in range(HEADS // 2)]
    s_bf = [s.astype(BF16) for s in s_cur]
    outs = [_dot(scores[h], vb[:, h * DV:(h + 1) * DV])
            + _dot(jnp.where(half_masks[h % 2], q_dec[:, pair_lanes[h // 2]], 0.0).astype(BF16), s_bf[h // 2])
            for h in range(HEADS)]
    for p in range(HEADS // 2):
        u = _dot(k_end[:, pair_lanes[p]].T.astype(BF16), vb[:, 2 * p * DV:(2 * p + 2) * DV])
        dec = jnp.where(sub < DK, jnp.exp(float(n) * log_g[2 * p]), jnp.exp(float(n) * log_g[2 * p + 1]))
        state_ref[0, p] = dec * s_cur[p] + _pair_blocks(u)
    return outs


def _softmax_rows(s):
    e = jnp.exp(s - jnp.max(s, axis=-1, keepdims=True))
    return e / jnp.sum(e, axis=-1, keepdims=True)


def _memory_attention_shared(xq, mk_ref, mv_ref):
    mkb = mk_ref[0].astype(BF16)
    mvb = mv_ref[0].astype(BF16)
    heads = [slice(h * DV, (h + 1) * DV) for h in range(HEADS)]
    scores = [_dot_nt(xq[:, hs].astype(BF16), mkb[:, hs]) * (DV ** -0.5) for hs in heads]
    probs = [_softmax_rows(s).astype(BF16) for s in scores]
    return [_dot(prob, mvb[:, hs]) for prob, hs in zip(probs, heads)]


def _memory_attention_per_seq(xq, rows, mk_ref, mv_ref):
    nseq = mk_ref.shape[0]
    r = lax.broadcasted_iota(jnp.int32, (HEADS * rows, MEM_LEN * HEADS), 0)
    c = lax.broadcasted_iota(jnp.int32, (HEADS * rows, MEM_LEN * HEADS), 1)
    own_head = (c % HEADS) == (r // rows)
    scores = []
    for i in range(nseq):
        qb = xq[i * rows:(i + 1) * rows]
        lhs = jnp.concatenate([qb[:, h * DV:(h + 1) * DV] for h in range(HEADS)], axis=0)
        scores.append(_dot_nt(lhs.astype(BF16), mk_ref[i].astype(BF16)) * (DV ** -0.5))
    probs = [_softmax_rows(jnp.where(own_head, s, -1e30)).astype(BF16) for s in scores]
    pv = [_dot(probs[i], mv_ref[i].astype(BF16)) for i in range(nseq)]
    return [jnp.concatenate([pv[i][h * rows:(h + 1) * rows] for i in range(nseq)], axis=0) for h in range(HEADS)]


def _mixer_kernel(x_ref, cos_ref, sin_ref, mk_ref, mv_ref, sg_in_ref, sr_in_ref, sg_buf_ref, sr_buf_ref,
                  n1g_ref, wa_ref, wb_ref, wga_ref, aup_ref, ab_ref, gon_ref, rgn_ref, rld_ref,
                  wbg_ref, wbr_ref, wbx_ref, bg_ref, wout_ref,
                  xo_ref, sg_ref, sr_ref, *, chunk, chain):
    del sg_buf_ref, sr_buf_ref
    n = x_ref.shape[0]

    @pl.when(pl.program_id(1) == 0)
    def _():
        sg_ref[...] = sg_in_ref[...]
        sr_ref[...] = sr_in_ref[...]

    x = x_ref[...]
    hb = _rms_norm(x, n1g_ref[...]).astype(BF16)

    def proj_a(off, width):
        return _dot(hb, wa_ref[:, off:off + width])

    def proj_b(off, width):
        return _dot(hb, wb_ref[:, off:off + width])

    row = lax.broadcasted_iota(jnp.int32, (n, n), 0)
    col = lax.broadcasted_iota(jnp.int32, (n, n), 1)
    tri = (((row // chunk) == (col // chunk)) & (col <= row)).astype(BF16)
    ga = _dot(hb, wga_ref[...]).astype(BF16)
    z = _dot(ga, aup_ref[...]) + ab_ref[...]
    lc = (jnp.minimum(z, 0.0) - jnp.log1p(jnp.exp(-jnp.abs(z)))) / GLA_TAU
    parts = _split3(lc)
    b = _dot(tri, parts[0]) + _dot(tri, parts[1]) + _dot(tri, parts[2])
    o_gla = _chunked_linear_attention(proj_a(A_GQ, QK_W) * (DK ** -0.5), proj_a(A_GK, QK_W), proj_a(A_GV, V_W),
                                      b, chunk, chain, sg_ref)

    lane = lax.broadcasted_iota(jnp.int32, (1, QK_W), 1)
    first_half = (lane % DK) < (DK // 2)
    cos = cos_ref[...]
    sin = sin_ref[...]

    def rope(t):
        rot = jnp.where(first_half, pltpu.roll(t, QK_W - DK // 2, 1), pltpu.roll(t, DK // 2, 1))
        return t * cos + rot * sin

    rq = rope(proj_b(B_RQ, QK_W))
    rk = rope(proj_b(B_RK, QK_W)) * (DK ** -0.5)
    rv = proj_b(B_RV, V_W)
    rld = rld_ref[...]
    if chain:
        o_ret = _retention_tile(rq, rk, rv, rld, sr_ref)
    else:
        t_in_chunk = (lax.broadcasted_iota(jnp.int32, (n, 1), 0) % chunk + 1).astype(F32)
        o_ret = _chunked_linear_attention(rq, rk, rv, t_in_chunk * rld, chunk, chain, sr_ref)

    xq = proj_b(B_XQ, V_W)
    if chain:
        o_xa = _memory_attention_shared(xq, mk_ref, mv_ref)
    else:
        o_xa = _memory_attention_per_seq(xq, chunk, mk_ref, mv_ref)

    gon = gon_ref[...]
    rgn = rgn_ref[...]
    for h in range(HEADS):
        hs = slice(h * DV, (h + 1) * DV)
        og = o_gla[h]
        o_gla[h] = og * lax.rsqrt(jnp.mean(og * og, axis=-1, keepdims=True) + NORM_EPS) * gon[:, hs]
        oc = o_ret[h] - jnp.mean(o_ret[h], axis=-1, keepdims=True)
        o_ret[h] = oc * lax.rsqrt(jnp.mean(oc * oc, axis=-1, keepdims=True) + GROUP_NORM_EPS) * rgn[:, hs]
    o_gla = (jnp.concatenate(o_gla, axis=1) * jax.nn.silu(proj_a(A_GR, V_W))).astype(BF16)
    o_ret = (jnp.concatenate(o_ret, axis=1) * jax.nn.silu(proj_b(B_RG, V_W))).astype(BF16)
    o_xa = jnp.concatenate(o_xa, axis=1).astype(BF16)

    def gate(j):
        return jax.nn.sigmoid(proj_b(B_GATES + j * D_MODEL, D_MODEL) + bg_ref[:, j * D_MODEL:(j + 1) * D_MODEL])

    merged = (gate(0) * _dot(o_gla, wbg_ref[...]) + gate(1) * _dot(o_ret, wbr_ref[...])
              + gate(2) * _dot(o_xa, wbx_ref[...]))
    xo_ref[...] = x + _dot(merged.astype(BF16), wout_ref[...])


def _ffn_kernel(x_ref, cb_in_ref, cb_buf_ref, n2g_ref, wup_ref, cw_ref, cbias_ref, wdown_ref, fg_ref,
                xo_ref, cb_ref, *, final_norm):
    del cb_buf_ref
    n = x_ref.shape[0]
    sb = cb_ref.shape[0]
    rows = n // sb

    @pl.when(pl.program_id(1) == 0)
    def _():
        cb_ref[...] = cb_in_ref[...]

    x = x_ref[...]
    hb = _rms_norm(x, n2g_ref[...]).astype(BF16)
    a = _dot(hb, wup_ref[:, :D_FF])
    gate_in = _dot(hb, wup_ref[:, D_FF:])

    cb = cb_ref[...]
    buf0 = jnp.broadcast_to(cb[:, 0:1, :], (sb, rows, D_FF)).reshape(n, D_FF)
    buf1 = jnp.broadcast_to(cb[:, 1:2, :], (sb, rows, D_FF)).reshape(n, D_FF)
    t = lax.broadcasted_iota(jnp.int32, (n, 1), 0) % rows
    prev1 = jnp.where(t == 0, buf1, pltpu.roll(a, 1, 0))
    prev2 = jnp.where(t == 0, buf0, jnp.where(t == 1, buf1, pltpu.roll(a, 2, 0)))
    cw = cw_ref[...]
    c = cbias_ref[...] + prev2 * cw[0:1] + prev1 * cw[1:2] + a * cw[2:3]
    cb_ref[...] = a.reshape(sb, rows, D_FF)[:, rows - (CONV_W - 1):, :]

    y = x + _dot((jax.nn.gelu(c) * gate_in).astype(BF16), wdown_ref[...])
    if final_norm:
        y = _rms_norm(y, fg_ref[...])
    xo_ref[...] = y


def _memkv_kernel(m_ref, g_ref, w_ref, k_ref, v_ref):
    hb = _rms_norm(m_ref[...], g_ref[...]).astype(BF16)
    k_ref[...] = _dot(hb, w_ref[:, :V_W])
    v_ref[...] = _dot(hb, w_ref[:, V_W:])


def _layer_resident(a, layer):
    zeros = (0,) * (a.ndim - 1)
    return pl.BlockSpec((None,) + a.shape[1:], lambda *_: (layer,) + zeros, pipeline_mode=pl.Buffered(1))


def _layer_blocks(a, layer, sb):
    zeros = (0,) * (a.ndim - 2)
    return pl.BlockSpec((None, sb) + a.shape[2:], lambda i, l: (layer, i) + zeros)


def _compiler_params():
    return pltpu.CompilerParams(dimension_semantics=("arbitrary", "arbitrary"),
                                vmem_limit_bytes=VMEM_LIMIT_BYTES)


_ANY = pl.BlockSpec(memory_space=pl.ANY)


def _mixer_call(x, cos, sin, mk, mv, sg_in, sr_in, sg_buf, sr_buf, w, *, layer, n, sb, nl, chunk, chain, name):
    tokens = x.shape[0]
    nb = tokens // (n * nl)
    x_spec = pl.BlockSpec((n, D_MODEL), lambda i, l: (i * nl + l, 0))
    pos_spec = pl.BlockSpec((n, QK_W), lambda i, l: (l, 0))
    weights = (w['norm1_g'], w['w_in_a'], w['w_in_b'], w['w_in_ga'], w['gla_a_up'], w['gla_a_b'],
               w['gla_onorm_g'], w['ret_gnorm_g'], w['ret_log_decay'], w['w_br_gla'], w['w_br_ret'],
               w['w_br_xa'], w['b_gate'], w['w_out'])
    return pl.pallas_call(
        functools.partial(_mixer_kernel, chunk=chunk, chain=chain),
        grid=(nb, nl),
        in_specs=[x_spec, pos_spec, pos_spec, _layer_blocks(mk, layer, sb), _layer_blocks(mv, layer, sb),
                  _layer_blocks(sg_in, layer, sb), _layer_blocks(sr_in, layer, sb), _ANY, _ANY]
                 + [_layer_resident(a, layer) for a in weights],
        out_specs=[x_spec, _layer_blocks(sg_buf, layer, sb), _layer_blocks(sr_buf, layer, sb)],
        out_shape=[jax.ShapeDtypeStruct(x.shape, F32), jax.ShapeDtypeStruct(sg_buf.shape, F32),
                   jax.ShapeDtypeStruct(sr_buf.shape, F32)],
        input_output_aliases={7: 1, 8: 2},
        compiler_params=_compiler_params(),
        name=name,
    )(x, cos, sin, mk, mv, sg_in, sr_in, sg_buf, sr_buf, *weights)


def _ffn_call(x, cb_in, cb_buf, w, *, layer, n, sb, nl, final_norm, name):
    tokens = x.shape[0]
    nb = tokens // (n * nl)
    x_spec = pl.BlockSpec((n, D_MODEL), lambda i, l: (i * nl + l, 0))
    weights = (w['norm2_g'], w['w_up'], w['conv_w'], w['conv_b'], w['w_down'], w['final_g'])
    return pl.pallas_call(
        functools.partial(_ffn_kernel, final_norm=final_norm),
        grid=(nb, nl),
        in_specs=[x_spec, _layer_blocks(cb_in, layer, sb), _ANY] + [_layer_resident(a, layer) for a in weights],
        out_specs=[x_spec, _layer_blocks(cb_buf, layer, sb)],
        out_shape=[jax.ShapeDtypeStruct(x.shape, F32), jax.ShapeDtypeStruct(cb_buf.shape, F32)],
        input_output_aliases={2: 1},
        compiler_params=_compiler_params(),
        name=name,
    )(x, cb_in, cb_buf, *weights)


def _memkv_call(mem, g, w):
    rows = mem.shape[0]
    tile = 512
    out = jax.ShapeDtypeStruct((DEPTH, rows, V_W), F32)
    kv_spec = pl.BlockSpec((None, tile, V_W), lambda d, r: (d, r, 0))
    return pl.pallas_call(
        _memkv_kernel,
        grid=(DEPTH, rows // tile),
        in_specs=[pl.BlockSpec((tile, D_MODEL), lambda d, r: (r, 0)),
                  pl.BlockSpec((None, 1, D_MODEL), lambda d, r: (d, 0, 0)),
                  pl.BlockSpec((None, D_MODEL, 2 * V_W), lambda d, r: (d, 0, 0))],
        out_specs=[kv_spec, kv_spec],
        out_shape=[out, out],
        compiler_params=_compiler_params(),
        name="memory_kv",
    )(mem, g, w)


def _rope_tables(pos):
    half = DK // 2
    inv = ROPE_BASE ** (-jnp.arange(half, dtype=F32) / half)
    ang = pos.astype(F32)[:, None] * inv[None, :]
    cos = jnp.cos(ang)
    sin = jnp.sin(ang)
    return (jnp.tile(jnp.concatenate([cos, cos], axis=-1), (1, HEADS)),
            jnp.tile(jnp.concatenate([-sin, sin], axis=-1), (1, HEADS)))


def kernel(x_prompt, x_sample, mem_prompt, state_gla, state_ret, state_ffn_conv, cache_mem_k, cache_mem_v, norm1_g, w_in, gla_a_up, gla_a_b, gla_onorm_g, ret_gnorm_g, mem_norm_g, w_mem_kv, w_br_gla, w_br_ret, w_br_xa, b_gate, w_out, norm2_g, w_up, conv_w, conv_b, w_down, final_g):
    bp, lp, _ = x_prompt.shape
    bs, ls, _ = x_sample.shape
    chunk_p = math.gcd(lp, MIX_CHUNK)
    chunk_s = math.gcd(ls, MIX_CHUNK)
    assert lp % PROMPT_TILE == 0 and PROMPT_TILE % chunk_p == 0 and lp % PROMPT_FFN_TILE == 0
    assert chunk_s == ls and bs % SAMPLE_SEQS == 0 and bs % SAMPLE_FFN_SEQS == 0 and ls >= CONV_W - 1

    ga_src = A_COLS
    w = {
        'norm1_g': norm1_g[:, None, :],
        'w_in_a': w_in[:, :, :ga_src].astype(BF16),
        'w_in_b': w_in[:, :, ga_src + GLA_LOWRANK:].astype(BF16),
        'w_in_ga': jnp.pad(w_in[:, :, ga_src:ga_src + GLA_LOWRANK].astype(BF16),
                           ((0, 0), (0, 0), (0, LANES - GLA_LOWRANK))),
        'gla_a_up': jnp.pad(gla_a_up.astype(BF16), ((0, 0), (0, LANES - GLA_LOWRANK), (0, 0))),
        'gla_a_b': gla_a_b[:, None, :], 'gla_onorm_g': gla_onorm_g[:, None, :],
        'ret_gnorm_g': ret_gnorm_g[:, None, :],
        'ret_log_decay': jnp.broadcast_to(
            jnp.repeat(jnp.log(1.0 - 2.0 ** (-5.0 - jnp.arange(HEADS, dtype=F32))), DK)[None, None, :],
            (DEPTH, 1, QK_W)),
        'w_br_gla': w_br_gla.astype(BF16), 'w_br_ret': w_br_ret.astype(BF16), 'w_br_xa': w_br_xa.astype(BF16),
        'b_gate': b_gate[:, None, :], 'w_out': w_out.astype(BF16),
        'norm2_g': norm2_g[:, None, :], 'w_up': w_up.astype(BF16), 'conv_w': conv_w,
        'conv_b': conv_b[:, None, :], 'w_down': w_down.astype(BF16),
        'final_g': jnp.broadcast_to(final_g[None, None, :], (DEPTH, 1, D_MODEL)),
    }

    cos_p, sin_p = _rope_tables(jnp.arange(lp, dtype=jnp.int32))
    cos_s, sin_s = _rope_tables(PAST_LEN + jnp.arange(ls, dtype=jnp.int32))
    cos_s = jnp.tile(cos_s, (SAMPLE_SEQS, 1))
    sin_s = jnp.tile(sin_s, (SAMPLE_SEQS, 1))

    mem_k, mem_v = _memkv_call(mem_prompt.reshape(bp * MEM_LEN, D_MODEL), mem_norm_g[:, None, :],
                               w_mem_kv.astype(BF16))
    mem_k = mem_k.reshape(DEPTH, bp, MEM_LEN, V_W)
    mem_v = mem_v.reshape(DEPTH, bp, MEM_LEN, V_W)
    cache_k = cache_mem_k.reshape(DEPTH, bs, MEM_LEN * HEADS, DV)
    cache_v = cache_mem_v.reshape(DEPTH, bs, MEM_LEN * HEADS, DV)

    pair_shape = (HEADS // 2, LANES, LANES)
    p_zero_state = jnp.zeros((DEPTH, bp) + pair_shape, F32)
    p_zero_conv = jnp.zeros((DEPTH, bp, CONV_W - 1, D_FF), F32)
    s_gla_in = state_gla.reshape((DEPTH, bs) + pair_shape)
    s_ret_in = state_ret.reshape((DEPTH, bs) + pair_shape)
    p_gla, p_ret, p_conv = p_zero_state, p_zero_state, p_zero_conv
    s_gla = jnp.zeros_like(s_gla_in)
    s_ret = jnp.zeros_like(s_ret_in)
    s_conv = jnp.zeros_like(state_ffn_conv)

    xp = x_prompt.reshape(bp * lp, D_MODEL)
    xs = x_sample.reshape(bs * ls, D_MODEL)
    for i in range(DEPTH):
        last = i == DEPTH - 1
        xp, p_gla, p_ret = _mixer_call(xp, cos_p, sin_p, mem_k, mem_v, p_zero_state, p_zero_state, p_gla, p_ret, w,
                                       layer=i, n=PROMPT_TILE, sb=1, nl=lp // PROMPT_TILE, chunk=chunk_p,
                                       chain=True, name=f"mixer_prompt_{i}")
        xp, p_conv = _ffn_call(xp, p_zero_conv, p_conv, w, layer=i, n=PROMPT_FFN_TILE, sb=1,
                               nl=lp // PROMPT_FFN_TILE,
                               final_norm=last, name=f"ffn_prompt_{i}")
        xs, s_gla, s_ret = _mixer_call(xs, cos_s, sin_s, cache_k, cache_v, s_gla_in, s_ret_in, s_gla, s_ret, w,
                                       layer=i, n=SAMPLE_SEQS * ls, sb=SAMPLE_SEQS, nl=1, chunk=chunk_s,
                                       chain=False, name=f"mixer_sample_{i}")
        xs, s_conv = _ffn_call(xs, state_ffn_conv, s_conv, w, layer=i, n=SAMPLE_FFN_SEQS * ls, sb=SAMPLE_FFN_SEQS, nl=1,
                               final_norm=last, name=f"ffn_sample_{i}")

    kv_shape = (DEPTH, bp, MEM_LEN, HEADS, DV)
    return (xp.reshape(bp, lp, D_MODEL), xs.reshape(bs, ls, D_MODEL),
            p_gla.reshape(DEPTH, bp, HEADS, DK, DV), p_ret.reshape(DEPTH, bp, HEADS, DK, DV), p_conv,
            mem_k.reshape(kv_shape), mem_v.reshape(kv_shape),
            s_gla.reshape(DEPTH, bs, HEADS, DK, DV), s_ret.reshape(DEPTH, bs, HEADS, DK, DV), s_conv)
```

```python
import functools
import math

import jax
import jax.numpy as jnp
from jax import lax
from jax.experimental import pallas as pl
from jax.experimental.pallas import tpu as pltpu

F32 = jnp.float32
BF16 = jnp.bfloat16

D_MODEL = 1024
DEPTH = 2
PAST_LEN = 16384
HEADS = 4
DK = 64
DV = 128
QK_W = HEADS * DK
V_W = HEADS * DV
GLA_LOWRANK = 16
GLA_TAU = 16.0
ROPE_BASE = 10000.0
MIX_CHUNK = 32
MEM_LEN = 256
D_FF = 2816
CONV_W = 3
NORM_EPS = 1e-6
GROUP_NORM_EPS = 1e-5
N_BRANCH = 3

LANES = 128
SUBLANES = 8
VMEM_LIMIT_BYTES = 56 * 1024 * 1024

A_GQ = 0
A_GK = A_GQ + QK_W
A_GV = A_GK + QK_W
A_GR = A_GV + V_W
A_COLS = A_GR + V_W
B_RQ = 0
B_RK = B_RQ + QK_W
B_RV = B_RK + QK_W
B_RG = B_RV + V_W
B_XQ = B_RG + V_W
B_GATES = B_XQ + V_W
B_COLS = B_GATES + N_BRANCH * D_MODEL

PROMPT_TILE = 512
PROMPT_SUB_TILE = 256
PROMPT_FFN_TILE = 512
SAMPLE_SEQS = 8
SAMPLE_FFN_SEQS = 32


def _dot(a, b):
    return jnp.dot(a, b, preferred_element_type=F32)


def _dot_nt(a, b):
    return lax.dot_general(a, b, (((1,), (1,)), ((), ())), preferred_element_type=F32)


def _rms_norm(x, g):
    return x * lax.rsqrt(jnp.mean(x * x, axis=-1, keepdims=True) + NORM_EPS) * g


def _split3(x):
    hi = x.astype(BF16)
    r = x - hi.astype(F32)
    mid = r.astype(BF16)
    lo = (r - mid.astype(F32)).astype(BF16)
    return hi, mid, lo


def _pad_rows(x, rows):
    if x.shape[0] == rows:
        return x
    return jnp.concatenate([x, jnp.zeros((rows - x.shape[0], x.shape[1]), x.dtype)], axis=0)


def _chunk_total(b, chunk):
    n, w = b.shape
    last = b.reshape(n // chunk, chunk, w)[:, chunk - 1:chunk, :]
    return jnp.broadcast_to(last, (n // chunk, chunk, w)).reshape(n, w)


def _half_masks():
    lane = lax.broadcasted_iota(jnp.int32, (1, LANES), 1)
    return (lane < DK, lane >= DK)


def _pair_blocks(u):
    sub = lax.broadcasted_iota(jnp.int32, (LANES, DV), 0)
    return jnp.where(sub < DK, u[:, :DV], u[:, DV:])


def _chunked_linear_attention(q, k, v, b, chunk, chain, state_ref):
    n = q.shape[0]
    nc = n // chunk
    npad = max(n, LANES)
    bl = _chunk_total(b, chunk)
    q_dec = q * jnp.exp(b)
    k_dec = k * jnp.exp(-b)
    k_end = k * jnp.exp(bl - b)
    vb = v.astype(BF16)
    vb_pad = _pad_rows(vb, npad)

    row = lax.broadcasted_iota(jnp.int32, (n, n), 0)
    col = lax.broadcasted_iota(jnp.int32, (n, n), 1)
    causal = ((row // chunk) == (col // chunk)) & (col <= row)
    half_masks = _half_masks()
    lane_t = lax.broadcasted_iota(jnp.int32, (LANES, npad), 1)

    pairs = range(HEADS // 2)
    chunks = range(nc)
    qm, intra, u_blk, dec = [], [], [], []
    for p in pairs:
        ps = slice(p * LANES, (p + 1) * LANES)
        qp = q_dec[:, ps]
        kpb = k_dec[:, ps].astype(BF16)
        qm.append([jnp.where(m, qp, 0.0) for m in half_masks])
        for half in range(2):
            hs = slice((2 * p + half) * DV, (2 * p + half + 1) * DV)
            s = _dot_nt(qm[p][half].astype(BF16), kpb)
            intra.append(_dot(jnp.where(causal, s, 0.0).astype(BF16), vb[:, hs]))
        ke_t = _pad_rows(k_end[:, ps], npad).T
        bl_t = _pad_rows(bl[:, ps], npad).T
        vp = vb_pad[:, 2 * p * DV:(2 * p + 2) * DV]
        u_blk.append([])
        dec.append([])
        for c in chunks:
            in_chunk = (lane_t >= c * chunk) & (lane_t < (c + 1) * chunk)
            u_blk[p].append(_pair_blocks(_dot(jnp.where(in_chunk, ke_t, 0.0).astype(BF16), vp)))
            dec[p].append(jnp.exp(bl_t[:, c * chunk:c * chunk + 1]))

    s_start = []
    for p in pairs:
        s_start.append([])
        if chain:
            s_cur = state_ref[0, p]
        for c in chunks:
            if not chain:
                s_cur = state_ref[c, p]
            s_start[p].append(s_cur.astype(BF16))
            s_cur = dec[p][c] * s_cur + u_blk[p][c]
            if not chain:
                state_ref[c, p] = s_cur
        if chain:
            state_ref[0, p] = s_cur

    outs = []
    for p in pairs:
        inter = ([], [])
        for c in chunks:
            rs = slice(c * chunk, (c + 1) * chunk)
            lhs = jnp.concatenate([qm[p][0][rs], qm[p][1][rs]], axis=0).astype(BF16)
            oi = _dot(lhs, s_start[p][c])
            inter[0].append(oi[:chunk])
            inter[1].append(oi[chunk:])
        outs.append(intra[2 * p] + jnp.concatenate(inter[0], axis=0))
        outs.append(intra[2 * p + 1] + jnp.concatenate(inter[1], axis=0))
    return outs


def _retention_tile(q, k, v, rld, state_ref):
    n = q.shape[0]
    t = lax.broadcasted_iota(jnp.int32, (n, 1), 0).astype(F32)
    q_dec = q * jnp.exp((t + 1.0) * rld)
    k_end = k * jnp.exp((float(n - 1) - t) * rld)
    vb = v.astype(BF16)
    row = lax.broadcasted_iota(jnp.int32, (n, n), 0)
    col = lax.broadcasted_iota(jnp.int32, (n, n), 1)
    causal = col <= row
    delta = (row - col).astype(F32)
    half_masks = _half_masks()
    sub = lax.broadcasted_iota(jnp.int32, (LANES, 1), 0)

    pair_lanes = [slice(p * LANES, (p + 1) * LANES) for p in range(HEADS // 2)]
    log_g = [rld[:, h * DK:h * DK + 1] for h in range(HEADS)]
    raw = [_dot_nt(jnp.where(half_masks[h % 2], q[:, pair_lanes[h // 2]], 0.0).astype(BF16),
                   k[:, pair_lanes[h // 2]].astype(BF16)) for h in range(HEADS)]
    scores = [(raw[h] * jnp.where(causal, jnp.exp(delta * log_g[h]), 0.0)).astype(BF16) for h in range(HEADS)]
    s_cur = [state_ref[0, p] for p in range(HEADS // 2)]
    s_bf = [s.astype(BF16) for s in s_cur]
    outs = [_dot(scores[h], vb[:, h * DV:(h + 1) * DV])
            + _dot(jnp.where(half_masks[h % 2], q_dec[:, pair_lanes[h // 2]], 0.0).astype(BF16), s_bf[h // 2])
            for h in range(HEADS)]
    for p in range(HEADS // 2):
        u = _dot(k_end[:, pair_lanes[p]].T.astype(BF16), vb[:, 2 * p * DV:(2 * p + 2) * DV])
        dec = jnp.where(sub < DK, jnp.exp(float(n) * log_g[2 * p]), jnp.exp(float(n) * log_g[2 * p + 1]))
        state_ref[0, p] = dec * s_cur[p] + _pair_blocks(u)
    return outs


def _softmax_rows(s):
    e = jnp.exp(s - jnp.max(s, axis=-1, keepdims=True))
    return e / jnp.sum(e, axis=-1, keepdims=True)


def _memory_attention_shared(xq, mk_ref, mv_ref):
    mkb = mk_ref[0].astype(BF16)
    mvb = mv_ref[0].astype(BF16)
    heads = [slice(h * DV, (h + 1) * DV) for h in range(HEADS)]
    scores = [_dot_nt(xq[:, hs].astype(BF16), mkb[:, hs]) * (DV ** -0.5) for hs in heads]
    probs = [_softmax_rows(s).astype(BF16) for s in scores]
    return [_dot(prob, mvb[:, hs]) for prob, hs in zip(probs, heads)]


def _memory_attention_per_seq(xq, rows, mk_ref, mv_ref):
    nseq = mk_ref.shape[0]
    r = lax.broadcasted_iota(jnp.int32, (HEADS * rows, MEM_LEN * HEADS), 0)
    c = lax.broadcasted_iota(jnp.int32, (HEADS * rows, MEM_LEN * HEADS), 1)
    own_head = (c % HEADS) == (r // rows)
    scores = []
    for i in range(nseq):
        qb = xq[i * rows:(i + 1) * rows]
        lhs = jnp.concatenate([qb[:, h * DV:(h + 1) * DV] for h in range(HEADS)], axis=0)
        scores.append(_dot_nt(lhs.astype(BF16), mk_ref[i].astype(BF16)) * (DV ** -0.5))
    probs = [_softmax_rows(jnp.where(own_head, s, -1e30)).astype(BF16) for s in scores]
    pv = [_dot(probs[i], mv_ref[i].astype(BF16)) for i in range(nseq)]
    return [jnp.concatenate([pv[i][h * rows:(h + 1) * rows] for i in range(nseq)], axis=0) for h in range(HEADS)]


def _split_stacked(out_ref, layer):
    if layer == 0:
        return out_ref.at[pl.ds(1, DEPTH - 1)], out_ref.at[0]
    return None, out_ref


def _zero_fill(ref):
    if ref is not None:
        ref[...] = jnp.zeros(ref.shape, ref.dtype)


def _mixer_kernel(x_ref, cos_ref, sin_ref, mk_ref, mv_ref, sg_in_ref, sr_in_ref, sg_buf_ref, sr_buf_ref,
                  n1g_ref, wa_ref, wb_ref, wga_ref, aup_ref, ab_ref, gon_ref, rgn_ref, rld_ref,
                  wbg_ref, wbr_ref, wbx_ref, bg_ref, wout_ref,
                  xo_ref, sg_ref, sr_ref, *, chunk, chain, sub, layer):
    del sg_buf_ref, sr_buf_ref
    n = x_ref.shape[0]
    sg_later, sg_ref = _split_stacked(sg_ref, layer)
    sr_later, sr_ref = _split_stacked(sr_ref, layer)

    @pl.when(pl.program_id(1) == 0)
    def _():
        sg_ref[...] = sg_in_ref[...]
        sr_ref[...] = sr_in_ref[...]
        _zero_fill(sg_later)
        _zero_fill(sr_later)

    x = x_ref[...]
    hb = _rms_norm(x, n1g_ref[...]).astype(BF16)

    def proj_a(off, width):
        return _dot(hb, wa_ref[:, off:off + width])

    def proj_b(off, width):
        return _dot(hb, wb_ref[:, off:off + width])

    sub_rows = [slice(j * sub, (j + 1) * sub) for j in range(n // sub)]

    def per_sub_tile(fn):
        outs = [fn(rs) for rs in sub_rows]
        return [jnp.concatenate([o[h] for o in outs], axis=0) for h in range(HEADS)]

    row = lax.broadcasted_iota(jnp.int32, (sub, sub), 0)
    col = lax.broadcasted_iota(jnp.int32, (sub, sub), 1)
    tri = (((row // chunk) == (col // chunk)) & (col <= row)).astype(BF16)
    ga = _dot(hb, wga_ref[...]).astype(BF16)
    z = _dot(ga, aup_ref[...]) + ab_ref[...]
    lc = (jnp.minimum(z, 0.0) - jnp.log1p(jnp.exp(-jnp.abs(z)))) / GLA_TAU
    gq = proj_a(A_GQ, QK_W) * (DK ** -0.5)
    gk = proj_a(A_GK, QK_W)
    gv = proj_a(A_GV, V_W)

    def gla(rs):
        parts = _split3(lc[rs])
        b = _dot(tri, parts[0]) + _dot(tri, parts[1]) + _dot(tri, parts[2])
        return _chunked_linear_attention(gq[rs], gk[rs], gv[rs], b, chunk, chain, sg_ref)

    o_gla = per_sub_tile(gla)

    lane = lax.broadcasted_iota(jnp.int32, (1, QK_W), 1)
    first_half = (lane % DK) < (DK // 2)
    cos = cos_ref[...]
    sin = sin_ref[...]

    def rope(t):
        rot = jnp.where(first_half, pltpu.roll(t, QK_W - DK // 2, 1), pltpu.roll(t, DK // 2, 1))
        return t * cos + rot * sin

    rq = rope(proj_b(B_RQ, QK_W))
    rk = rope(proj_b(B_RK, QK_W)) * (DK ** -0.5)
    rv = proj_b(B_RV, V_W)
    rld = rld_ref[...]
    if chain:
        o_ret = per_sub_tile(lambda rs: _retention_tile(rq[rs], rk[rs], rv[rs], rld, sr_ref))
    else:
        t_in_chunk = (lax.broadcasted_iota(jnp.int32, (sub, 1), 0) % chunk + 1).astype(F32)
        o_ret = per_sub_tile(lambda rs: _chunked_linear_attention(rq[rs], rk[rs], rv[rs], t_in_chunk * rld,
                                                                  chunk, chain, sr_ref))

    xq = proj_b(B_XQ, V_W)
    if chain:
        o_xa = _memory_attention_shared(xq, mk_ref, mv_ref)
    else:
        o_xa = _memory_attention_per_seq(xq, chunk, mk_ref, mv_ref)

    gon = gon_ref[...]
    rgn = rgn_ref[...]
    for h in range(HEADS):
        hs = slice(h * DV, (h + 1) * DV)
        og = o_gla[h]
        o_gla[h] = og * lax.rsqrt(jnp.mean(og * og, axis=-1, keepdims=True) + NORM_EPS) * gon[:, hs]
        oc = o_ret[h] - jnp.mean(o_ret[h], axis=-1, keepdims=True)
        o_ret[h] = oc * lax.rsqrt(jnp.mean(oc * oc, axis=-1, keepdims=True) + GROUP_NORM_EPS) * rgn[:, hs]
    o_gla = (jnp.concatenate(o_gla, axis=1) * jax.nn.silu(proj_a(A_GR, V_W))).astype(BF16)
    o_ret = (jnp.concatenate(o_ret, axis=1) * jax.nn.silu(proj_b(B_RG, V_W))).astype(BF16)
    o_xa = jnp.concatenate(o_xa, axis=1).astype(BF16)

    def gate(j):
        return jax.nn.sigmoid(proj_b(B_GATES + j * D_MODEL, D_MODEL) + bg_ref[:, j * D_MODEL:(j + 1) * D_MODEL])

    merged = (gate(0) * _dot(o_gla, wbg_ref[...]) + gate(1) * _dot(o_ret, wbr_ref[...])
              + gate(2) * _dot(o_xa, wbx_ref[...]))
    xo_ref[...] = x + _dot(merged.astype(BF16), wout_ref[...])


def _ffn_kernel(x_ref, cb_in_ref, cb_buf_ref, n2g_ref, wup_ref, cw_ref, cbias_ref, wdown_ref, fg_ref,
                xo_ref, cb_ref, *, final_norm, layer):
    del cb_buf_ref
    n = x_ref.shape[0]
    cb_later, cb_ref = _split_stacked(cb_ref, layer)
    sb = cb_ref.shape[0]
    rows = n // sb

    @pl.when(pl.program_id(1) == 0)
    def _():
        cb_ref[...] = cb_in_ref[...]
        _zero_fill(cb_later)

    x = x_ref[...]
    hb = _rms_norm(x, n2g_ref[...]).astype(BF16)
    a = _dot(hb, wup_ref[:, :D_FF])
    gate_in = _dot(hb, wup_ref[:, D_FF:])

    cb = cb_ref[...]
    buf0 = jnp.broadcast_to(cb[:, 0:1, :], (sb, rows, D_FF)).reshape(n, D_FF)
    buf1 = jnp.broadcast_to(cb[:, 1:2, :], (sb, rows, D_FF)).reshape(n, D_FF)
    t = lax.broadcasted_iota(jnp.int32, (n, 1), 0) % rows
    prev1 = jnp.where(t == 0, buf1, pltpu.roll(a, 1, 0))
    prev2 = jnp.where(t == 0, buf0, jnp.where(t == 1, buf1, pltpu.roll(a, 2, 0)))
    cw = cw_ref[...]
    c = cbias_ref[...] + prev2 * cw[0:1] + prev1 * cw[1:2] + a * cw[2:3]
    cb_ref[...] = a.reshape(sb, rows, D_FF)[:, rows - (CONV_W - 1):, :]

    y = x + _dot((jax.nn.gelu(c) * gate_in).astype(BF16), wdown_ref[...])
    if final_norm:
        y = _rms_norm(y, fg_ref[...])
    xo_ref[...] = y


def _memkv_kernel(m_ref, g_ref, w_ref, k_ref, v_ref, k_rows_ref, v_rows_ref):
    tile = m_ref.shape[0]
    hb = _rms_norm(m_ref[...], g_ref[...]).astype(BF16)
    for col0, slot_ref, rows_ref in ((0, k_ref, k_rows_ref), (V_W, v_ref, v_rows_ref)):
        val = _dot(hb, w_ref[:, col0:col0 + V_W])
        slot_ref[...] = val
        for h in range(HEADS):
            rows_ref[pl.ds(h, tile, stride=HEADS), :] = val[:, h * DV:(h + 1) * DV]


def _layer_resident(a, layer):
    zeros = (0,) * (a.ndim - 1)
    return pl.BlockSpec((None,) + a.shape[1:], lambda *_: (layer,) + zeros, pipeline_mode=pl.Buffered(1))


def _layer_blocks(a, layer, sb):
    zeros = (0,) * (a.ndim - 2)
    return pl.BlockSpec((None, sb) + a.shape[2:], lambda i, l: (layer, i) + zeros)


def _stacked_out_blocks(a, layer, sb):
    if layer > 0:
        return _layer_blocks(a, layer, sb)
    zeros = (0,) * (a.ndim - 2)
    return pl.BlockSpec((a.shape[0], sb) + a.shape[2:], lambda i, l: (0, i) + zeros)


def _compiler_params():
    return pltpu.CompilerParams(dimension_semantics=("arbitrary", "arbitrary"),
                                vmem_limit_bytes=VMEM_LIMIT_BYTES)


_ANY = pl.BlockSpec(memory_space=pl.ANY)


def _mixer_call(x, cos, sin, mk, mv, sg_in, sr_in, sg_buf, sr_buf, w, *, layer, n, sub, sb, nl, chunk, chain,
                name):
    tokens = x.shape[0]
    nb = tokens // (n * nl)
    x_spec = pl.BlockSpec((n, D_MODEL), lambda i, l: (i * nl + l, 0))
    pos_spec = pl.BlockSpec((n, QK_W), lambda i, l: (l, 0))
    weights = (w['norm1_g'], w['w_in_a'], w['w_in_b'], w['w_in_ga'], w['gla_a_up'], w['gla_a_b'],
               w['gla_onorm_g'], w['ret_gnorm_g'], w['ret_log_decay'], w['w_br_gla'], w['w_br_ret'],
               w['w_br_xa'], w['b_gate'], w['w_out'])
    return pl.pallas_call(
        functools.partial(_mixer_kernel, chunk=chunk, chain=chain, sub=sub, layer=layer),
        grid=(nb, nl),
        in_specs=[x_spec, pos_spec, pos_spec, _layer_blocks(mk, layer, sb), _layer_blocks(mv, layer, sb),
                  _layer_blocks(sg_in, layer, sb), _layer_blocks(sr_in, layer, sb), _ANY, _ANY]
                 + [_layer_resident(a, layer) for a in weights],
        out_specs=[x_spec, _stacked_out_blocks(sg_buf, layer, sb), _stacked_out_blocks(sr_buf, layer, sb)],
        out_shape=[jax.ShapeDtypeStruct(x.shape, F32), jax.ShapeDtypeStruct(sg_buf.shape, F32),
                   jax.ShapeDtypeStruct(sr_buf.shape, F32)],
        input_output_aliases={7: 1, 8: 2} if layer > 0 else {},
        compiler_params=_compiler_params(),
        name=name,
    )(x, cos, sin, mk, mv, sg_in, sr_in, sg_buf, sr_buf, *weights)


def _ffn_call(x, cb_in, cb_buf, w, *, layer, n, sb, nl, final_norm, name):
    tokens = x.shape[0]
    nb = tokens // (n * nl)
    x_spec = pl.BlockSpec((n, D_MODEL), lambda i, l: (i * nl + l, 0))
    weights = (w['norm2_g'], w['w_up'], w['conv_w'], w['conv_b'], w['w_down'], w['final_g'])
    return pl.pallas_call(
        functools.partial(_ffn_kernel, final_norm=final_norm, layer=layer),
        grid=(nb, nl),
        in_specs=[x_spec, _layer_blocks(cb_in, layer, sb), _ANY] + [_layer_resident(a, layer) for a in weights],
        out_specs=[x_spec, _stacked_out_blocks(cb_buf, layer, sb)],
        out_shape=[jax.ShapeDtypeStruct(x.shape, F32), jax.ShapeDtypeStruct(cb_buf.shape, F32)],
        input_output_aliases={2: 1} if layer > 0 else {},
        compiler_params=_compiler_params(),
        name=name,
    )(x, cb_in, cb_buf, *weights)


def _transpose_cast_kernel(wt_ref, o_ref):
    o_ref[...] = wt_ref[0].T.astype(BF16)


def _in_proj_section(w_in_t, first, cols, block):
    return pl.pallas_call(
        _transpose_cast_kernel,
        grid=(DEPTH, cols // block),
        in_specs=[pl.BlockSpec((pl.Element(1), pl.Element(block), pl.Element(D_MODEL)),
                               lambda d, j: (d, pl.multiple_of(first + j * block, SUBLANES), 0))],
        out_specs=pl.BlockSpec((None, D_MODEL, block), lambda d, j: (d, 0, j)),
        out_shape=jax.ShapeDtypeStruct((DEPTH, D_MODEL, cols), BF16),
        compiler_params=_compiler_params(),
        name=f"in_proj_cols_{first}",
    )(w_in_t)


def _memkv_call(mem, g, w):
    rows = mem.shape[0]
    tile = 512
    out = jax.ShapeDtypeStruct((DEPTH, rows, V_W), F32)
    out_rows = jax.ShapeDtypeStruct((DEPTH, rows * HEADS, DV), F32)
    kv_spec = pl.BlockSpec((None, tile, V_W), lambda d, r: (d, r, 0))
    kv_rows_spec = pl.BlockSpec((None, tile * HEADS, DV), lambda d, r: (d, r, 0))
    return pl.pallas_call(
        _memkv_kernel,
        grid=(DEPTH, rows // tile),
        in_specs=[pl.BlockSpec((tile, D_MODEL), lambda d, r: (r, 0)),
                  pl.BlockSpec((None, 1, D_MODEL), lambda d, r: (d, 0, 0)),
                  pl.BlockSpec((None, D_MODEL, 2 * V_W), lambda d, r: (d, 0, 0))],
        out_specs=[kv_spec, kv_spec, kv_rows_spec, kv_rows_spec],
        out_shape=[out, out, out_rows, out_rows],
        compiler_params=_compiler_params(),
        name="memory_kv",
    )(mem, g, w)


def _rope_tables(pos):
    half = DK // 2
    inv = ROPE_BASE ** (-jnp.arange(half, dtype=F32) / half)
    ang = pos.astype(F32)[:, None] * inv[None, :]
    cos = jnp.cos(ang)
    sin = jnp.sin(ang)
    return (jnp.tile(jnp.concatenate([cos, cos], axis=-1), (1, HEADS)),
            jnp.tile(jnp.concatenate([-sin, sin], axis=-1), (1, HEADS)))


def kernel(x_prompt, x_sample, mem_prompt, state_gla, state_ret, state_ffn_conv, cache_mem_k, cache_mem_v, norm1_g, w_in, gla_a_up, gla_a_b, gla_onorm_g, ret_gnorm_g, mem_norm_g, w_mem_kv, w_br_gla, w_br_ret, w_br_xa, b_gate, w_out, norm2_g, w_up, conv_w, conv_b, w_down, final_g):
    bp, lp, _ = x_prompt.shape
    bs, ls, _ = x_sample.shape
    chunk_p = math.gcd(lp, MIX_CHUNK)
    chunk_s = math.gcd(ls, MIX_CHUNK)
    assert lp % PROMPT_TILE == 0 and PROMPT_TILE % PROMPT_SUB_TILE == 0 and PROMPT_SUB_TILE % chunk_p == 0
    assert lp % PROMPT_FFN_TILE == 0
    assert chunk_s == ls and bs % SAMPLE_SEQS == 0 and bs % SAMPLE_FFN_SEQS == 0 and ls >= CONV_W - 1

    w_in_t = jnp.swapaxes(w_in, 1, 2)
    w = {
        'norm1_g': norm1_g[:, None, :],
        'w_in_a': _in_proj_section(w_in_t, 0, A_COLS, 2 * LANES),
        'w_in_b': _in_proj_section(w_in_t, A_COLS + GLA_LOWRANK, B_COLS, 2 * LANES),
        'w_in_ga': _in_proj_section(w_in_t, A_COLS, LANES, LANES),
        'gla_a_up': jnp.pad(gla_a_up.astype(BF16), ((0, 0), (0, LANES - GLA_LOWRANK), (0, 0))),
        'gla_a_b': gla_a_b[:, None, :], 'gla_onorm_g': gla_onorm_g[:, None, :],
        'ret_gnorm_g': ret_gnorm_g[:, None, :],
        'ret_log_decay': jnp.broadcast_to(
            jnp.repeat(jnp.log(1.0 - 2.0 ** (-5.0 - jnp.arange(HEADS, dtype=F32))), DK)[None, None, :],
            (DEPTH, 1, QK_W)),
        'w_br_gla': w_br_gla.astype(BF16), 'w_br_ret': w_br_ret.astype(BF16), 'w_br_xa': w_br_xa.astype(BF16),
        'b_gate': b_gate[:, None, :], 'w_out': w_out.astype(BF16),
        'norm2_g': norm2_g[:, None, :], 'w_up': w_up.astype(BF16), 'conv_w': conv_w,
        'conv_b': conv_b[:, None, :], 'w_down': w_down.astype(BF16),
        'final_g': jnp.broadcast_to(final_g[None, None, :], (DEPTH, 1, D_MODEL)),
    }

    cos_p, sin_p = _rope_tables(jnp.arange(lp, dtype=jnp.int32))
    cos_s, sin_s = _rope_tables(PAST_LEN + jnp.arange(ls, dtype=jnp.int32))
    cos_s = jnp.tile(cos_s, (SAMPLE_SEQS, 1))
    sin_s = jnp.tile(sin_s, (SAMPLE_SEQS, 1))

    mem_k, mem_v, mem_k_rows, mem_v_rows = _memkv_call(mem_prompt.reshape(bp * MEM_LEN, D_MODEL), mem_norm_g[:, None, :],
                               w_mem_kv.astype(BF16))
    mem_k = mem_k.reshape(DEPTH, bp, MEM_LEN, V_W)
    mem_v = mem_v.reshape(DEPTH, bp, MEM_LEN, V_W)
    cache_k = cache_mem_k.reshape(DEPTH, bs, MEM_LEN * HEADS, DV)
    cache_v = cache_mem_v.reshape(DEPTH, bs, MEM_LEN * HEADS, DV)

    pair_shape = (HEADS // 2, LANES, LANES)
    p_zero_state = jnp.zeros((DEPTH, bp) + pair_shape, F32)
    p_zero_conv = jnp.zeros((DEPTH, bp, CONV_W - 1, D_FF), F32)
    s_gla_in = state_gla.reshape((DEPTH, bs) + pair_shape)
    s_ret_in = state_ret.reshape((DEPTH, bs) + pair_shape)
    p_gla, p_ret, p_conv = p_zero_state, p_zero_state, p_zero_conv
    s_gla, s_ret, s_conv = s_gla_in, s_ret_in, state_ffn_conv

    xp = x_prompt.reshape(bp * lp, D_MODEL)
    xs = x_sample.reshape(bs * ls, D_MODEL)
    for i in range(DEPTH):
        last = i == DEPTH - 1
        xp, p_gla, p_ret = _mixer_call(xp, cos_p, sin_p, mem_k, mem_v, p_zero_state, p_zero_state, p_gla, p_ret, w,
                                       layer=i, n=PROMPT_TILE, sub=PROMPT_SUB_TILE, sb=1, nl=lp // PROMPT_TILE,
                                       chunk=chunk_p,
                                       chain=True, name=f"mixer_prompt_{i}")
        xp, p_conv = _ffn_call(xp, p_zero_conv, p_conv, w, layer=i, n=PROMPT_FFN_TILE, sb=1,
                               nl=lp // PROMPT_FFN_TILE,
                               final_norm=last, name=f"ffn_prompt_{i}")
        xs, s_gla, s_ret = _mixer_call(xs, cos_s, sin_s, cache_k, cache_v, s_gla_in, s_ret_in, s_gla, s_ret, w,
                                       layer=i, n=SAMPLE_SEQS * ls, sub=SAMPLE_SEQS * ls, sb=SAMPLE_SEQS, nl=1,
                                       chunk=chunk_s,
                                       chain=False, name=f"mixer_sample_{i}")
        xs, s_conv = _ffn_call(xs, state_ffn_conv, s_conv, w, layer=i, n=SAMPLE_FFN_SEQS * ls, sb=SAMPLE_FFN_SEQS, nl=1,
                               final_norm=last, name=f"ffn_sample_{i}")

    kv_shape = (DEPTH, bp, MEM_LEN, HEADS, DV)
    return (xp.reshape(bp, lp, D_MODEL), xs.reshape(bs, ls, D_MODEL),
            p_gla.reshape(DEPTH, bp, HEADS, DK, DV), p_ret.reshape(DEPTH, bp, HEADS, DK, DV), p_conv,
            mem_k_rows.reshape(kv_shape), mem_v_rows.reshape(kv_shape),
            s_gla.reshape(DEPTH, bs, HEADS, DK, DV), s_ret.reshape(DEPTH, bs, HEADS, DK, DV), s_conv)
```

```python
import functools
import math

import jax
import jax.numpy as jnp
from jax import lax
from jax.experimental import pallas as pl
from jax.experimental.pallas import tpu as pltpu

F32 = jnp.float32
BF16 = jnp.bfloat16

D_MODEL = 1024
DEPTH = 2
PAST_LEN = 16384
HEADS = 4
DK = 64
DV = 128
QK_W = HEADS * DK
V_W = HEADS * DV
GLA_LOWRANK = 16
GLA_TAU = 16.0
ROPE_BASE = 10000.0
MIX_CHUNK = 32
MEM_LEN = 256
D_FF = 2816
CONV_W = 3
NORM_EPS = 1e-6
GROUP_NORM_EPS = 1e-5
N_BRANCH = 3

LANES = 128
SUBLANES = 8
VMEM_LIMIT_BYTES = 56 * 1024 * 1024

A_GQ = 0
A_GK = A_GQ + QK_W
A_GV = A_GK + QK_W
A_GR = A_GV + V_W
A_COLS = A_GR + V_W
B_RQ = 0
B_RK = B_RQ + QK_W
B_RV = B_RK + QK_W
B_RG = B_RV + V_W
B_XQ = B_RG + V_W
B_GATES = B_XQ + V_W
B_COLS = B_GATES + N_BRANCH * D_MODEL

PROMPT_SEQS = 1
PROMPT_TILE = 512
PROMPT_SUB_TILE = 256
PROMPT_FFN_TILE = 512
SAMPLE_SEQS = 8
SAMPLE_FFN_SEQS = 32
IN_PROJ_PREP_BLOCK = 512


def _dot(a, b):
    return jnp.dot(a, b, preferred_element_type=F32)


def _dot_nt(a, b):
    return lax.dot_general(a, b, (((1,), (1,)), ((), ())), preferred_element_type=F32)


def _rms_norm(x, g):
    return x * lax.rsqrt(jnp.mean(x * x, axis=-1, keepdims=True) + NORM_EPS) * g


def _split3(x):
    hi = x.astype(BF16)
    r = x - hi.astype(F32)
    mid = r.astype(BF16)
    lo = (r - mid.astype(F32)).astype(BF16)
    return hi, mid, lo


def _pad_rows(x, rows):
    if x.shape[0] == rows:
        return x
    return jnp.concatenate([x, jnp.zeros((rows - x.shape[0], x.shape[1]), x.dtype)], axis=0)


def _chunk_last(b, chunk):
    n, w = b.shape
    return b.reshape(n // chunk, chunk, w)[:, chunk - 1:chunk, :]


def _half_masks():
    lane = lax.broadcasted_iota(jnp.int32, (1, LANES), 1)
    return (lane < DK, lane >= DK)


def _pair_blocks(u):
    sub = lax.broadcasted_iota(jnp.int32, (LANES, DV), 0)
    return jnp.where(sub < DK, u[:, :DV], u[:, DV:])


def _chunked_linear_attention(q, k, v, b, chunk, chain, state_ref):
    n = q.shape[0]
    nc = n // chunk
    npad = max(n, LANES)
    last = _chunk_last(b, chunk)
    bl = jnp.broadcast_to(last, (nc, chunk, b.shape[1])).reshape(n, b.shape[1])
    last = last.reshape(nc, b.shape[1])
    q_dec = q * jnp.exp(b)
    k_dec = k * jnp.exp(-b)
    k_end = k * jnp.exp(bl - b)
    vb = v.astype(BF16)
    vb_pad = _pad_rows(vb, npad)

    row = lax.broadcasted_iota(jnp.int32, (n, n), 0)
    col = lax.broadcasted_iota(jnp.int32, (n, n), 1)
    causal = ((row // chunk) == (col // chunk)) & (col <= row)
    half_masks = _half_masks()
    lane_t = lax.broadcasted_iota(jnp.int32, (LANES, npad), 1)

    pairs = range(HEADS // 2)
    chunks = range(nc)
    qm, intra, u_blk, dec = [], [], [], []
    for p in pairs:
        ps = slice(p * LANES, (p + 1) * LANES)
        qp = q_dec[:, ps]
        kpb = k_dec[:, ps].astype(BF16)
        qm.append([jnp.where(m, qp, 0.0) for m in half_masks])
        for half in range(2):
            hs = slice((2 * p + half) * DV, (2 * p + half + 1) * DV)
            s = _dot_nt(qm[p][half].astype(BF16), kpb)
            intra.append(_dot(jnp.where(causal, s, 0.0).astype(BF16), vb[:, hs]))
        ke_t = _pad_rows(k_end[:, ps], npad).T
        total_t = _pad_rows(last[:, ps], LANES).T
        vp = vb_pad[:, 2 * p * DV:(2 * p + 2) * DV]
        u_blk.append([])
        dec.append([])
        for c in chunks:
            in_chunk = (lane_t >= c * chunk) & (lane_t < (c + 1) * chunk)
            u_blk[p].append(_pair_blocks(_dot(jnp.where(in_chunk, ke_t, 0.0).astype(BF16), vp)))
            dec[p].append(jnp.exp(total_t[:, c:c + 1]))

    s_start = []
    for p in pairs:
        s_start.append([])
        if chain:
            s_cur = state_ref[0, p]
        for c in chunks:
            if not chain:
                s_cur = state_ref[c, p]
            s_start[p].append(s_cur.astype(BF16))
            s_cur = dec[p][c] * s_cur + u_blk[p][c]
            if not chain:
                state_ref[c, p] = s_cur
        if chain:
            state_ref[0, p] = s_cur

    outs = []
    for p in pairs:
        inter = ([], [])
        for c in chunks:
            rs = slice(c * chunk, (c + 1) * chunk)
            lhs = jnp.concatenate([qm[p][0][rs], qm[p][1][rs]], axis=0).astype(BF16)
            oi = _dot(lhs, s_start[p][c])
            inter[0].append(oi[:chunk])
            inter[1].append(oi[chunk:])
        outs.append(intra[2 * p] + jnp.concatenate(inter[0], axis=0))
        outs.append(intra[2 * p + 1] + jnp.concatenate(inter[1], axis=0))
    return outs


def _retention_tile(q, k, v, rld, state_ref):
    n = q.shape[0]
    t = lax.broadcasted_iota(jnp.int32, (n, 1), 0).astype(F32)
    q_dec = q * jnp.exp((t + 1.0) * rld)
    k_end = k * jnp.exp((float(n - 1) - t) * rld)
    vb = v.astype(BF16)
    row = lax.broadcasted_iota(jnp.int32, (n, n), 0)
    col = lax.broadcasted_iota(jnp.int32, (n, n), 1)
    causal = col <= row
    delta = (row - col).astype(F32)
    half_masks = _half_masks()
    sub = lax.broadcasted_iota(jnp.int32, (LANES, 1), 0)

    pair_lanes = [slice(p * LANES, (p + 1) * LANES) for p in range(HEADS // 2)]
    log_g = [rld[:, h * DK:h * DK + 1] for h in range(HEADS)]
    raw = [_dot_nt(jnp.where(half_masks[h % 2], q[:, pair_lanes[h // 2]], 0.0).astype(BF16),
                   k[:, pair_lanes[h // 2]].astype(BF16)) for h in range(HEADS)]
    scores = [(raw[h] * jnp.where(causal, jnp.exp(delta * log_g[h]), 0.0)).astype(BF16) for h in range(HEADS)]
    s_cur = [state_ref[0, p] for p in range(HEADS // 2)]
    s_bf = [s.astype(BF16) for s in s_cur]
    outs = [_dot(scores[h], vb[:, h * DV:(h + 1) * DV])
            + _dot(jnp.where(half_masks[h % 2], q_dec[:, pair_lanes[h // 2]], 0.0).astype(BF16), s_bf[h // 2])
            for h in range(HEADS)]
    for p in range(HEADS // 2):
        u = _dot(k_end[:, pair_lanes[p]].T.astype(BF16), vb[:, 2 * p * DV:(2 * p + 2) * DV])
        dec = jnp.where(sub < DK, jnp.exp(float(n) * log_g[2 * p]), jnp.exp(float(n) * log_g[2 * p + 1]))
        state_ref[0, p] = dec * s_cur[p] + _pair_blocks(u)
    return outs


def _softmax_rows(s):
    e = jnp.exp(s - jnp.max(s, axis=-1, keepdims=True))
    return e / jnp.sum(e, axis=-1, keepdims=True)


def _memory_attention_shared(xq, mk_ref, mv_ref):
    mkb = mk_ref[0].astype(BF16)
    mvb = mv_ref[0].astype(BF16)
    heads = [slice(h * DV, (h + 1) * DV) for h in range(HEADS)]
    scores = [_dot_nt(xq[:, hs].astype(BF16), mkb[:, hs]) * (DV ** -0.5) for hs in heads]
    probs = [_softmax_rows(s).astype(BF16) for s in scores]
    return [_dot(prob, mvb[:, hs]) for prob, hs in zip(probs, heads)]


def _memory_attention_per_seq(xq, rows, mk_ref, mv_ref):
    nseq = mk_ref.shape[0]
    r = lax.broadcasted_iota(jnp.int32, (HEADS * rows, MEM_LEN * HEADS), 0)
    c = lax.broadcasted_iota(jnp.int32, (HEADS * rows, MEM_LEN * HEADS), 1)
    own_head = (c % HEADS) == (r // rows)
    scores = []
    for i in range(nseq):
        qb = xq[i * rows:(i + 1) * rows]
        lhs = jnp.concatenate([qb[:, h * DV:(h + 1) * DV] for h in range(HEADS)], axis=0)
        scores.append(_dot_nt(lhs.astype(BF16), mk_ref[i].astype(BF16)) * (DV ** -0.5))
    probs = [_softmax_rows(jnp.where(own_head, s, -1e30)).astype(BF16) for s in scores]
    pv = [_dot(probs[i], mv_ref[i].astype(BF16)) for i in range(nseq)]
    return [jnp.concatenate([pv[i][h * rows:(h + 1) * rows] for i in range(nseq)], axis=0) for h in range(HEADS)]


def _split_stacked(out_ref, layer):
    if layer == 0:
        return out_ref.at[pl.ds(1, DEPTH - 1)], out_ref.at[0]
    return None, out_ref


def _zero_fill(ref):
    if ref is not None:
        ref[...] = jnp.zeros(ref.shape, ref.dtype)


def _mixer_kernel(x_ref, cos_ref, sin_ref, mk_ref, mv_ref, sg_in_ref, sr_in_ref, sg_buf_ref, sr_buf_ref,
                  n1g_ref, wa_ref, wb_ref, wga_ref, aup_ref, ab_ref, gon_ref, rgn_ref, rld_ref,
                  wbg_ref, wbr_ref, wbx_ref, bg_ref, wout_ref,
                  xo_ref, sg_ref, sr_ref, *, chunk, chain, sub, layer):
    del sg_buf_ref, sr_buf_ref
    seqs, rows, _ = x_ref.shape
    n = seqs * rows
    if not chain:
        sub = n
    sg_later, sg_ref = _split_stacked(sg_ref, layer)
    sr_later, sr_ref = _split_stacked(sr_ref, layer)

    @pl.when(pl.program_id(1) == 0)
    def _():
        sg_ref[...] = sg_in_ref[...]
        sr_ref[...] = sr_in_ref[...]
        _zero_fill(sg_later)
        _zero_fill(sr_later)

    x = x_ref[...].reshape(n, D_MODEL)
    hb = _rms_norm(x, n1g_ref[...]).astype(BF16)

    def proj_a(off, width):
        return _dot(hb, wa_ref[:, off:off + width])

    def proj_b(off, width):
        return _dot(hb, wb_ref[:, off:off + width])

    sub_tiles = range(n // sub)
    sub_rows = [slice(j * sub, (j + 1) * sub) for j in sub_tiles]

    def own(ref, j):
        return ref.at[pl.ds(j * sub // rows, 1)] if chain else ref

    def stack_sub_tiles(outs):
        return [jnp.concatenate([o[h] for o in outs], axis=0) for h in range(HEADS)]

    row = lax.broadcasted_iota(jnp.int32, (sub, sub), 0)
    col = lax.broadcasted_iota(jnp.int32, (sub, sub), 1)
    tri = (((row // chunk) == (col // chunk)) & (col <= row)).astype(BF16)
    ga = _dot(hb, wga_ref[...]).astype(BF16)
    z = _dot(ga, aup_ref[...]) + ab_ref[...]
    lc = (jnp.minimum(z, 0.0) - jnp.log1p(jnp.exp(-jnp.abs(z)))) / GLA_TAU
    gq = proj_a(A_GQ, QK_W) * (DK ** -0.5)
    gk = proj_a(A_GK, QK_W)
    gv = proj_a(A_GV, V_W)

    def gla(j):
        rs = sub_rows[j]
        parts = _split3(lc[rs])
        b = _dot(tri, parts[0]) + _dot(tri, parts[1]) + _dot(tri, parts[2])
        return _chunked_linear_attention(gq[rs], gk[rs], gv[rs], b, chunk, chain, own(sg_ref, j))

    lane = lax.broadcasted_iota(jnp.int32, (1, QK_W), 1)
    first_half = (lane % DK) < (DK // 2)
    cos = jnp.concatenate([cos_ref[...]] * seqs, axis=0)
    sin = jnp.concatenate([sin_ref[...]] * seqs, axis=0)

    def rope(t):
        rot = jnp.where(first_half, pltpu.roll(t, QK_W - DK // 2, 1), pltpu.roll(t, DK // 2, 1))
        return t * cos + rot * sin

    rq = rope(proj_b(B_RQ, QK_W))
    rk = rope(proj_b(B_RK, QK_W)) * (DK ** -0.5)
    rv = proj_b(B_RV, V_W)
    rld = rld_ref[...]
    t_in_chunk = (lax.broadcasted_iota(jnp.int32, (sub, 1), 0) % chunk + 1).astype(F32)

    def ret(j):
        rs = sub_rows[j]
        if chain:
            return _retention_tile(rq[rs], rk[rs], rv[rs], rld, own(sr_ref, j))
        return _chunked_linear_attention(rq[rs], rk[rs], rv[rs], t_in_chunk * rld, chunk, chain, sr_ref)

    o_gla = stack_sub_tiles([gla(j) for j in sub_tiles])
    o_ret = stack_sub_tiles([ret(j) for j in sub_tiles])

    xq = proj_b(B_XQ, V_W)
    if chain:
        o_xa = stack_sub_tiles([_memory_attention_shared(xq[j * rows:(j + 1) * rows], mk_ref.at[pl.ds(j, 1)],
                                                         mv_ref.at[pl.ds(j, 1)]) for j in range(seqs)])
    else:
        o_xa = _memory_attention_per_seq(xq, chunk, mk_ref, mv_ref)

    gon = gon_ref[...]
    rgn = rgn_ref[...]
    for h in range(HEADS):
        hs = slice(h * DV, (h + 1) * DV)
        og = o_gla[h]
        o_gla[h] = og * lax.rsqrt(jnp.mean(og * og, axis=-1, keepdims=True) + NORM_EPS) * gon[:, hs]
        oc = o_ret[h] - jnp.mean(o_ret[h], axis=-1, keepdims=True)
        o_ret[h] = oc * lax.rsqrt(jnp.mean(oc * oc, axis=-1, keepdims=True) + GROUP_NORM_EPS) * rgn[:, hs]
    o_gla = (jnp.concatenate(o_gla, axis=1) * jax.nn.silu(proj_a(A_GR, V_W))).astype(BF16)
    o_ret = (jnp.concatenate(o_ret, axis=1) * jax.nn.silu(proj_b(B_RG, V_W))).astype(BF16)
    o_xa = jnp.concatenate(o_xa, axis=1).astype(BF16)

    def gate(j):
        return jax.nn.sigmoid(proj_b(B_GATES + j * D_MODEL, D_MODEL) + bg_ref[:, j * D_MODEL:(j + 1) * D_MODEL])

    merged = (gate(0) * _dot(o_gla, wbg_ref[...]) + gate(1) * _dot(o_ret, wbr_ref[...])
              + gate(2) * _dot(o_xa, wbx_ref[...]))
    xo_ref[...] = (x + _dot(merged.astype(BF16), wout_ref[...])).reshape(seqs, rows, D_MODEL)


def _ffn_kernel(x_ref, cb_in_ref, cb_buf_ref, n2g_ref, wup_ref, cw_ref, cbias_ref, wdown_ref, fg_ref,
                xo_ref, cb_ref, *, final_norm, layer):
    del cb_buf_ref
    n = x_ref.shape[0]
    cb_later, cb_ref = _split_stacked(cb_ref, layer)
    sb = cb_ref.shape[0]
    rows = n // sb

    @pl.when(pl.program_id(1) == 0)
    def _():
        cb_ref[...] = cb_in_ref[...]
        _zero_fill(cb_later)

    x = x_ref[...]
    hb = _rms_norm(x, n2g_ref[...]).astype(BF16)
    a = _dot(hb, wup_ref[:, :D_FF])
    gate_in = _dot(hb, wup_ref[:, D_FF:])

    cb = cb_ref[...]
    buf0 = jnp.broadcast_to(cb[:, 0:1, :], (sb, rows, D_FF)).reshape(n, D_FF)
    buf1 = jnp.broadcast_to(cb[:, 1:2, :], (sb, rows, D_FF)).reshape(n, D_FF)
    t = lax.broadcasted_iota(jnp.int32, (n, 1), 0) % rows
    prev1 = jnp.where(t == 0, buf1, pltpu.roll(a, 1, 0))
    prev2 = jnp.where(t == 0, buf0, jnp.where(t == 1, buf1, pltpu.roll(a, 2, 0)))
    cw = cw_ref[...]
    c = cbias_ref[...] + prev2 * cw[0:1] + prev1 * cw[1:2] + a * cw[2:3]
    cb_ref[...] = a.reshape(sb, rows, D_FF)[:, rows - (CONV_W - 1):, :]

    y = x + _dot((jax.nn.gelu(c) * gate_in).astype(BF16), wdown_ref[...])
    if final_norm:
        y = _rms_norm(y, fg_ref[...])
    xo_ref[...] = y


def _memkv_kernel(m_ref, g_ref, w_ref, k_ref, v_ref, k_rows_ref, v_rows_ref):
    tile = m_ref.shape[0]
    hb = _rms_norm(m_ref[...], g_ref[...]).astype(BF16)
    for col0, slot_ref, rows_ref in ((0, k_ref, k_rows_ref), (V_W, v_ref, v_rows_ref)):
        val = _dot(hb, w_ref[:, col0:col0 + V_W])
        slot_ref[...] = val
        for h in range(HEADS):
            rows_ref[pl.ds(h, tile, stride=HEADS), :] = val[:, h * DV:(h + 1) * DV]


def _layer_resident(a, layer):
    zeros = (0,) * (a.ndim - 1)
    return pl.BlockSpec((None,) + a.shape[1:], lambda *_: (layer,) + zeros, pipeline_mode=pl.Buffered(1))


def _layer_blocks(a, layer, sb):
    zeros = (0,) * (a.ndim - 2)
    return pl.BlockSpec((None, sb) + a.shape[2:], lambda i, l: (layer, i) + zeros)


def _stacked_out_blocks(a, layer, sb):
    if layer > 0:
        return _layer_blocks(a, layer, sb)
    zeros = (0,) * (a.ndim - 2)
    return pl.BlockSpec((a.shape[0], sb) + a.shape[2:], lambda i, l: (0, i) + zeros)


def _compiler_params():
    return pltpu.CompilerParams(dimension_semantics=("arbitrary", "arbitrary"),
                                vmem_limit_bytes=VMEM_LIMIT_BYTES)


_ANY = pl.BlockSpec(memory_space=pl.ANY)


def _mixer_call(x, cos, sin, mk, mv, sg_in, sr_in, sg_buf, sr_buf, w, *, layer, sb, rows, sub, chunk, chain,
                name):
    nseq, length, _ = x.shape
    x_spec = pl.BlockSpec((sb, rows, D_MODEL), lambda i, l: (i, l, 0))
    pos_spec = pl.BlockSpec((rows, QK_W), lambda i, l: (l, 0))
    weights = (w['norm1_g'], w['w_in_a'], w['w_in_b'], w['w_in_ga'], w['gla_a_up'], w['gla_a_b'],
               w['gla_onorm_g'], w['ret_gnorm_g'], w['ret_log_decay'], w['w_br_gla'], w['w_br_ret'],
               w['w_br_xa'], w['b_gate'], w['w_out'])
    return pl.pallas_call(
        functools.partial(_mixer_kernel, chunk=chunk, chain=chain, sub=sub, layer=layer),
        grid=(nseq // sb, length // rows),
        in_specs=[x_spec, pos_spec, pos_spec, _layer_blocks(mk, layer, sb), _layer_blocks(mv, layer, sb),
                  _layer_blocks(sg_in, layer, sb), _layer_blocks(sr_in, layer, sb), _ANY, _ANY]
                 + [_layer_resident(a, layer) for a in weights],
        out_specs=[x_spec, _stacked_out_blocks(sg_buf, layer, sb), _stacked_out_blocks(sr_buf, layer, sb)],
        out_shape=[jax.ShapeDtypeStruct(x.shape, F32), jax.ShapeDtypeStruct(sg_buf.shape, F32),
                   jax.ShapeDtypeStruct(sr_buf.shape, F32)],
        input_output_aliases={7: 1, 8: 2} if layer > 0 else {},
        compiler_params=_compiler_params(),
        name=name,
    )(x, cos, sin, mk, mv, sg_in, sr_in, sg_buf, sr_buf, *weights)


def _ffn_call(x, cb_in, cb_buf, w, *, layer, n, sb, nl, final_norm, name):
    tokens = x.shape[0]
    nb = tokens // (n * nl)
    x_spec = pl.BlockSpec((n, D_MODEL), lambda i, l: (i * nl + l, 0))
    weights = (w['norm2_g'], w['w_up'], w['conv_w'], w['conv_b'], w['w_down'], w['final_g'])
    return pl.pallas_call(
        functools.partial(_ffn_kernel, final_norm=final_norm, layer=layer),
        grid=(nb, nl),
        in_specs=[x_spec, _layer_blocks(cb_in, layer, sb), _ANY] + [_layer_resident(a, layer) for a in weights],
        out_specs=[x_spec, _stacked_out_blocks(cb_buf, layer, sb)],
        out_shape=[jax.ShapeDtypeStruct(x.shape, F32), jax.ShapeDtypeStruct(cb_buf.shape, F32)],
        input_output_aliases={2: 1} if layer > 0 else {},
        compiler_params=_compiler_params(),
        name=name,
    )(x, cb_in, cb_buf, *weights)


def _transpose_cast_kernel(wt_ref, o_ref):
    o_ref[...] = wt_ref[0].T.astype(BF16)


def _in_proj_section(w_in_t, first, cols, block):
    return pl.pallas_call(
        _transpose_cast_kernel,
        grid=(DEPTH, cols // block),
        in_specs=[pl.BlockSpec((pl.Element(1), pl.Element(block), pl.Element(D_MODEL)),
                               lambda d, j: (d, pl.multiple_of(first + j * block, SUBLANES), 0))],
        out_specs=pl.BlockSpec((None, D_MODEL, block), lambda d, j: (d, 0, j)),
        out_shape=jax.ShapeDtypeStruct((DEPTH, D_MODEL, cols), BF16),
        compiler_params=_compiler_params(),
        name=f"in_proj_cols_{first}",
    )(w_in_t)


def _memkv_call(mem, g, w):
    rows = mem.shape[0]
    tile = 512
    out = jax.ShapeDtypeStruct((DEPTH, rows, V_W), F32)
    out_rows = jax.ShapeDtypeStruct((DEPTH, rows * HEADS, DV), F32)
    kv_spec = pl.BlockSpec((None, tile, V_W), lambda d, r: (d, r, 0))
    kv_rows_spec = pl.BlockSpec((None, tile * HEADS, DV), lambda d, r: (d, r, 0))
    return pl.pallas_call(
        _memkv_kernel,
        grid=(DEPTH, rows // tile),
        in_specs=[pl.BlockSpec((tile, D_MODEL), lambda d, r: (r, 0)),
                  pl.BlockSpec((None, 1, D_MODEL), lambda d, r: (d, 0, 0)),
                  pl.BlockSpec((None, D_MODEL, 2 * V_W), lambda d, r: (d, 0, 0))],
        out_specs=[kv_spec, kv_spec, kv_rows_spec, kv_rows_spec],
        out_shape=[out, out, out_rows, out_rows],
        compiler_params=_compiler_params(),
        name="memory_kv",
    )(mem, g, w)


def _rope_tables(pos):
    half = DK // 2
    inv = ROPE_BASE ** (-jnp.arange(half, dtype=F32) / half)
    ang = pos.astype(F32)[:, None] * inv[None, :]
    cos = jnp.cos(ang)
    sin = jnp.sin(ang)
    return (jnp.tile(jnp.concatenate([cos, cos], axis=-1), (1, HEADS)),
            jnp.tile(jnp.concatenate([-sin, sin], axis=-1), (1, HEADS)))


def kernel(x_prompt, x_sample, mem_prompt, state_gla, state_ret, state_ffn_conv, cache_mem_k, cache_mem_v, norm1_g, w_in, gla_a_up, gla_a_b, gla_onorm_g, ret_gnorm_g, mem_norm_g, w_mem_kv, w_br_gla, w_br_ret, w_br_xa, b_gate, w_out, norm2_g, w_up, conv_w, conv_b, w_down, final_g):
    bp, lp, _ = x_prompt.shape
    bs, ls, _ = x_sample.shape
    chunk_p = math.gcd(lp, MIX_CHUNK)
    chunk_s = math.gcd(ls, MIX_CHUNK)
    assert bp % PROMPT_SEQS == 0 and lp % PROMPT_TILE == 0
    assert PROMPT_TILE % PROMPT_SUB_TILE == 0 and PROMPT_SUB_TILE % chunk_p == 0
    assert lp % PROMPT_FFN_TILE == 0
    assert chunk_s == ls and bs % SAMPLE_SEQS == 0 and bs % SAMPLE_FFN_SEQS == 0 and ls >= CONV_W - 1

    w_in_t = jnp.swapaxes(w_in, 1, 2)
    w = {
        'norm1_g': norm1_g[:, None, :],
        'w_in_a': _in_proj_section(w_in_t, 0, A_COLS, IN_PROJ_PREP_BLOCK),
        'w_in_b': _in_proj_section(w_in_t, A_COLS + GLA_LOWRANK, B_COLS, IN_PROJ_PREP_BLOCK),
        'w_in_ga': _in_proj_section(w_in_t, A_COLS, LANES, LANES),
        'gla_a_up': jnp.pad(gla_a_up.astype(BF16), ((0, 0), (0, LANES - GLA_LOWRANK), (0, 0))),
        'gla_a_b': gla_a_b[:, None, :], 'gla_onorm_g': gla_onorm_g[:, None, :],
        'ret_gnorm_g': ret_gnorm_g[:, None, :],
        'ret_log_decay': jnp.broadcast_to(
            jnp.repeat(jnp.log(1.0 - 2.0 ** (-5.0 - jnp.arange(HEADS, dtype=F32))), DK)[None, None, :],
            (DEPTH, 1, QK_W)),
        'w_br_gla': w_br_gla.astype(BF16), 'w_br_ret': w_br_ret.astype(BF16), 'w_br_xa': w_br_xa.astype(BF16),
        'b_gate': b_gate[:, None, :], 'w_out': w_out.astype(BF16),
        'norm2_g': norm2_g[:, None, :], 'w_up': w_up.astype(BF16), 'conv_w': conv_w,
        'conv_b': conv_b[:, None, :], 'w_down': w_down.astype(BF16),
        'final_g': jnp.broadcast_to(final_g[None, None, :], (DEPTH, 1, D_MODEL)),
    }

    cos_p, sin_p = _rope_tables(jnp.arange(lp, dtype=jnp.int32))
    cos_s, sin_s = _rope_tables(PAST_LEN + jnp.arange(ls, dtype=jnp.int32))

    mem_k, mem_v, mem_k_rows, mem_v_rows = _memkv_call(mem_prompt.reshape(bp * MEM_LEN, D_MODEL), mem_norm_g[:, None, :],
                               w_mem_kv.astype(BF16))
    mem_k = mem_k.reshape(DEPTH, bp, MEM_LEN, V_W)
    mem_v = mem_v.reshape(DEPTH, bp, MEM_LEN, V_W)
    cache_k = cache_mem_k.reshape(DEPTH, bs, MEM_LEN * HEADS, DV)
    cache_v = cache_mem_v.reshape(DEPTH, bs, MEM_LEN * HEADS, DV)

    pair_shape = (HEADS // 2, LANES, LANES)
    p_zero_state = jnp.zeros((DEPTH, bp) + pair_shape, F32)
    p_zero_conv = jnp.zeros((DEPTH, bp, CONV_W - 1, D_FF), F32)
    s_gla_in = state_gla.reshape((DEPTH, bs) + pair_shape)
    s_ret_in = state_ret.reshape((DEPTH, bs) + pair_shape)
    p_gla, p_ret, p_conv = p_zero_state, p_zero_state, p_zero_conv
    s_gla, s_ret, s_conv = s_gla_in, s_ret_in, state_ffn_conv

    xp, xs = x_prompt, x_sample
    for i in range(DEPTH):
        last = i == DEPTH - 1
        xp, p_gla, p_ret = _mixer_call(xp, cos_p, sin_p, mem_k, mem_v, p_zero_state, p_zero_state, p_gla, p_ret, w,
                                       layer=i, sb=PROMPT_SEQS, rows=PROMPT_TILE, sub=PROMPT_SUB_TILE, chunk=chunk_p,
                                       chain=True, name=f"mixer_prompt_{i}")
        xp, p_conv = _ffn_call(xp.reshape(bp * lp, D_MODEL), p_zero_conv, p_conv, w, layer=i, n=PROMPT_FFN_TILE,
                               sb=1, nl=lp // PROMPT_FFN_TILE, final_norm=last, name=f"ffn_prompt_{i}")
        xp = xp.reshape(bp, lp, D_MODEL)
        xs, s_gla, s_ret = _mixer_call(xs, cos_s, sin_s, cache_k, cache_v, s_gla_in, s_ret_in, s_gla, s_ret, w,
                                       layer=i, sb=SAMPLE_SEQS, rows=ls, sub=ls, chunk=chunk_s, chain=False,
                                       name=f"mixer_sample_{i}")
        xs, s_conv = _ffn_call(xs.reshape(bs * ls, D_MODEL), state_ffn_conv, s_conv, w, layer=i,
                               n=SAMPLE_FFN_SEQS * ls, sb=SAMPLE_FFN_SEQS, nl=1, final_norm=last,
                               name=f"ffn_sample_{i}")
        xs = xs.reshape(bs, ls, D_MODEL)

    kv_shape = (DEPTH, bp, MEM_LEN, HEADS, DV)
    return (xp, xs,
            p_gla.reshape(DEPTH, bp, HEADS, DK, DV), p_ret.reshape(DEPTH, bp, HEADS, DK, DV), p_conv,
            mem_k_rows.reshape(kv_shape), mem_v_rows.reshape(kv_shape),
            s_gla.reshape(DEPTH, bs, HEADS, DK, DV), s_ret.reshape(DEPTH, bs, HEADS, DK, DV), s_conv)
```

```python
import functools
import math

import jax
import jax.numpy as jnp
from jax import lax
from jax.experimental import pallas as pl
from jax.experimental.pallas import tpu as pltpu

F32 = jnp.float32
BF16 = jnp.bfloat16

D_MODEL = 1024
DEPTH = 2
PAST_LEN = 16384
HEADS = 4
DK = 64
DV = 128
QK_W = HEADS * DK
V_W = HEADS * DV
GLA_LOWRANK = 16
GLA_TAU = 16.0
ROPE_BASE = 10000.0
MIX_CHUNK = 32
MEM_LEN = 256
D_FF = 2816
CONV_W = 3
NORM_EPS = 1e-6
GROUP_NORM_EPS = 1e-5
N_BRANCH = 3

LANES = 128
SUBLANES = 8
VMEM_LIMIT_BYTES = 56 * 1024 * 1024

A_GQ = 0
A_GK = A_GQ + QK_W
A_GV = A_GK + QK_W
A_GR = A_GV + V_W
A_COLS = A_GR + V_W
B_RQ = 0
B_RK = B_RQ + QK_W
B_RV = B_RK + QK_W
B_RG = B_RV + V_W
B_XQ = B_RG + V_W
B_GATES = B_XQ + V_W
B_COLS = B_GATES + N_BRANCH * D_MODEL

PROMPT_SEQS = 1
PROMPT_TILE = 512
PROMPT_SUB_TILE = 256
PROMPT_FFN_TILE = 1024
SAMPLE_SEQS = 8
SAMPLE_FFN_SEQS = 64
IN_PROJ_PREP_BLOCK = 512


def _dot(a, b):
    return jnp.dot(a, b, preferred_element_type=F32)


def _dot_nt(a, b):
    return lax.dot_general(a, b, (((1,), (1,)), ((), ())), preferred_element_type=F32)


def _rms_norm(x, g):
    return x * lax.rsqrt(jnp.mean(x * x, axis=-1, keepdims=True) + NORM_EPS) * g


def _split3(x):
    hi = x.astype(BF16)
    r = x - hi.astype(F32)
    mid = r.astype(BF16)
    lo = (r - mid.astype(F32)).astype(BF16)
    return hi, mid, lo


def _pad_rows(x, rows):
    if x.shape[0] == rows:
        return x
    return jnp.concatenate([x, jnp.zeros((rows - x.shape[0], x.shape[1]), x.dtype)], axis=0)


def _chunk_last(b, chunk):
    n, w = b.shape
    return b.reshape(n // chunk, chunk, w)[:, chunk - 1:chunk, :]


def _half_masks():
    lane = lax.broadcasted_iota(jnp.int32, (1, LANES), 1)
    return (lane < DK, lane >= DK)


def _pair_blocks(u):
    sub = lax.broadcasted_iota(jnp.int32, (LANES, DV), 0)
    return jnp.where(sub < DK, u[:, :DV], u[:, DV:])


def _chunked_linear_attention(q, k, v, b, chunk, chain, state_ref):
    n = q.shape[0]
    nc = n // chunk
    npad = max(n, LANES)
    last = _chunk_last(b, chunk)
    bl = jnp.broadcast_to(last, (nc, chunk, b.shape[1])).reshape(n, b.shape[1])
    last = last.reshape(nc, b.shape[1])
    q_dec = q * jnp.exp(b)
    k_dec = k * jnp.exp(-b)
    k_end = k * jnp.exp(bl - b)
    vb = v.astype(BF16)
    vb_pad = _pad_rows(vb, npad)

    row = lax.broadcasted_iota(jnp.int32, (n, n), 0)
    col = lax.broadcasted_iota(jnp.int32, (n, n), 1)
    causal = ((row // chunk) == (col // chunk)) & (col <= row)
    half_masks = _half_masks()
    lane_t = lax.broadcasted_iota(jnp.int32, (LANES, npad), 1)

    pairs = range(HEADS // 2)
    chunks = range(nc)
    qm, intra, u_blk, dec = [], [], [], []
    for p in pairs:
        ps = slice(p * LANES, (p + 1) * LANES)
        qp = q_dec[:, ps]
        kpb = k_dec[:, ps].astype(BF16)
        qm.append([jnp.where(m, qp, 0.0) for m in half_masks])
        for half in range(2):
            hs = slice((2 * p + half) * DV, (2 * p + half + 1) * DV)
            s = _dot_nt(qm[p][half].astype(BF16), kpb)
            intra.append(_dot(jnp.where(causal, s, 0.0).astype(BF16), vb[:, hs]))
        ke_t = _pad_rows(k_end[:, ps], npad).T
        total_t = _pad_rows(last[:, ps], LANES).T
        vp = vb_pad[:, 2 * p * DV:(2 * p + 2) * DV]
        u_blk.append([])
        dec.append([])
        for c in chunks:
            in_chunk = (lane_t >= c * chunk) & (lane_t < (c + 1) * chunk)
            u_blk[p].append(_pair_blocks(_dot(jnp.where(in_chunk, ke_t, 0.0).astype(BF16), vp)))
            dec[p].append(jnp.exp(total_t[:, c:c + 1]))

    s_start = []
    for p in pairs:
        s_start.append([])
        if chain:
            s_cur = state_ref[0, p]
        for c in chunks:
            if not chain:
                s_cur = state_ref[c, p]
            s_start[p].append(s_cur.astype(BF16))
            s_cur = dec[p][c] * s_cur + u_blk[p][c]
            if not chain:
                state_ref[c, p] = s_cur
        if chain:
            state_ref[0, p] = s_cur

    outs = []
    for p in pairs:
        inter = ([], [])
        for c in chunks:
            rs = slice(c * chunk, (c + 1) * chunk)
            lhs = jnp.concatenate([qm[p][0][rs], qm[p][1][rs]], axis=0).astype(BF16)
            oi = _dot(lhs, s_start[p][c])
            inter[0].append(oi[:chunk])
            inter[1].append(oi[chunk:])
        outs.append(intra[2 * p] + jnp.concatenate(inter[0], axis=0))
        outs.append(intra[2 * p + 1] + jnp.concatenate(inter[1], axis=0))
    return outs


def _retention_tile(q, k, v, rld, state_ref):
    n = q.shape[0]
    t = lax.broadcasted_iota(jnp.int32, (n, 1), 0).astype(F32)
    q_dec = q * jnp.exp((t + 1.0) * rld)
    k_end = k * jnp.exp((float(n - 1) - t) * rld)
    vb = v.astype(BF16)
    row = lax.broadcasted_iota(jnp.int32, (n, n), 0)
    col = lax.broadcasted_iota(jnp.int32, (n, n), 1)
    causal = col <= row
    delta = (row - col).astype(F32)
    half_masks = _half_masks()
    sub = lax.broadcasted_iota(jnp.int32, (LANES, 1), 0)

    pair_lanes = [slice(p * LANES, (p + 1) * LANES) for p in range(HEADS // 2)]
    log_g = [rld[:, h * DK:h * DK + 1] for h in range(HEADS)]
    raw = [_dot_nt(jnp.where(half_masks[h % 2], q[:, pair_lanes[h // 2]], 0.0).astype(BF16),
                   k[:, pair_lanes[h // 2]].astype(BF16)) for h in range(HEADS)]
    scores = [(raw[h] * jnp.where(causal, jnp.exp(delta * log_g[h]), 0.0)).astype(BF16) for h in range(HEADS)]
    s_cur = [state_ref[0, p] for p in range(HEADS // 2)]
    s_bf = [s.astype(BF16) for s in s_cur]
    outs = [_dot(scores[h], vb[:, h * DV:(h + 1) * DV])
            + _dot(jnp.where(half_masks[h % 2], q_dec[:, pair_lanes[h // 2]], 0.0).astype(BF16), s_bf[h // 2])
            for h in range(HEADS)]
    for p in range(HEADS // 2):
        u = _dot(k_end[:, pair_lanes[p]].T.astype(BF16), vb[:, 2 * p * DV:(2 * p + 2) * DV])
        dec = jnp.where(sub < DK, jnp.exp(float(n) * log_g[2 * p]), jnp.exp(float(n) * log_g[2 * p + 1]))
        state_ref[0, p] = dec * s_cur[p] + _pair_blocks(u)
    return outs


def _softmax_rows(s):
    e = jnp.exp(s - jnp.max(s, axis=-1, keepdims=True))
    return e / jnp.sum(e, axis=-1, keepdims=True)


def _memory_attention_shared(xq, mk_ref, mv_ref):
    mkb = mk_ref[0].astype(BF16)
    mvb = mv_ref[0].astype(BF16)
    heads = [slice(h * DV, (h + 1) * DV) for h in range(HEADS)]
    scores = [_dot_nt(xq[:, hs].astype(BF16), mkb[:, hs]) * (DV ** -0.5) for hs in heads]
    probs = [_softmax_rows(s).astype(BF16) for s in scores]
    return [_dot(prob, mvb[:, hs]) for prob, hs in zip(probs, heads)]


def _memory_attention_per_seq(xq, rows, mk_ref, mv_ref):
    nseq = mk_ref.shape[0]
    r = lax.broadcasted_iota(jnp.int32, (HEADS * rows, MEM_LEN * HEADS), 0)
    c = lax.broadcasted_iota(jnp.int32, (HEADS * rows, MEM_LEN * HEADS), 1)
    own_head = (c % HEADS) == (r // rows)
    scores = []
    for i in range(nseq):
        qb = xq[i * rows:(i + 1) * rows]
        lhs = jnp.concatenate([qb[:, h * DV:(h + 1) * DV] for h in range(HEADS)], axis=0)
        scores.append(_dot_nt(lhs.astype(BF16), mk_ref[i].astype(BF16)) * (DV ** -0.5))
    probs = [_softmax_rows(jnp.where(own_head, s, -1e30)).astype(BF16) for s in scores]
    pv = [_dot(probs[i], mv_ref[i].astype(BF16)) for i in range(nseq)]
    return [jnp.concatenate([pv[i][h * rows:(h + 1) * rows] for i in range(nseq)], axis=0) for h in range(HEADS)]


def _split_stacked(out_ref, layer):
    if layer == 0:
        return out_ref.at[pl.ds(1, DEPTH - 1)], out_ref.at[0]
    return None, out_ref


def _zero_fill(ref):
    if ref is not None:
        ref[...] = jnp.zeros(ref.shape, ref.dtype)


def _mixer_kernel(x_ref, cos_ref, sin_ref, mk_ref, mv_ref, sg_in_ref, sr_in_ref, sg_buf_ref, sr_buf_ref,
                  n1g_ref, wa_ref, wb_ref, wga_ref, aup_ref, ab_ref, gon_ref, rgn_ref, rld_ref,
                  wbg_ref, wbr_ref, wbx_ref, bg_ref, wout_ref,
                  xo_ref, sg_ref, sr_ref, *, chunk, chain, sub, layer):
    del sg_buf_ref, sr_buf_ref
    seqs, rows, _ = x_ref.shape
    n = seqs * rows
    if not chain:
        sub = n
    sg_later, sg_ref = _split_stacked(sg_ref, layer)
    sr_later, sr_ref = _split_stacked(sr_ref, layer)

    @pl.when(pl.program_id(1) == 0)
    def _():
        sg_ref[...] = sg_in_ref[...]
        sr_ref[...] = sr_in_ref[...]
        _zero_fill(sg_later)
        _zero_fill(sr_later)

    x = x_ref[...].reshape(n, D_MODEL)
    hb = _rms_norm(x, n1g_ref[...]).astype(BF16)

    def proj_a(off, width):
        return _dot(hb, wa_ref[:, off:off + width])

    def proj_b(off, width):
        return _dot(hb, wb_ref[:, off:off + width])

    sub_tiles = range(n // sub)
    sub_rows = [slice(j * sub, (j + 1) * sub) for j in sub_tiles]

    def own(ref, j):
        return ref.at[pl.ds(j * sub // rows, 1)] if chain else ref

    def stack_sub_tiles(outs):
        return [jnp.concatenate([o[h] for o in outs], axis=0) for h in range(HEADS)]

    row = lax.broadcasted_iota(jnp.int32, (sub, sub), 0)
    col = lax.broadcasted_iota(jnp.int32, (sub, sub), 1)
    tri = (((row // chunk) == (col // chunk)) & (col <= row)).astype(BF16)
    ga = _dot(hb, wga_ref[...]).astype(BF16)
    z = _dot(ga, aup_ref[...]) + ab_ref[...]
    lc = (jnp.minimum(z, 0.0) - jnp.log1p(jnp.exp(-jnp.abs(z)))) / GLA_TAU
    gq = proj_a(A_GQ, QK_W) * (DK ** -0.5)
    gk = proj_a(A_GK, QK_W)
    gv = proj_a(A_GV, V_W)

    def gla(j):
        rs = sub_rows[j]
        parts = _split3(lc[rs])
        b = _dot(tri, parts[0]) + _dot(tri, parts[1]) + _dot(tri, parts[2])
        return _chunked_linear_attention(gq[rs], gk[rs], gv[rs], b, chunk, chain, own(sg_ref, j))

    lane = lax.broadcasted_iota(jnp.int32, (1, QK_W), 1)
    first_half = (lane % DK) < (DK // 2)
    cos = jnp.concatenate([cos_ref[...]] * seqs, axis=0)
    sin = jnp.concatenate([sin_ref[...]] * seqs, axis=0)

    def rope(t):
        rot = jnp.where(first_half, pltpu.roll(t, QK_W - DK // 2, 1), pltpu.roll(t, DK // 2, 1))
        return t * cos + rot * sin

    rq = rope(proj_b(B_RQ, QK_W))
    rk = rope(proj_b(B_RK, QK_W)) * (DK ** -0.5)
    rv = proj_b(B_RV, V_W)
    rld = rld_ref[...]
    t_in_chunk = (lax.broadcasted_iota(jnp.int32, (sub, 1), 0) % chunk + 1).astype(F32)

    def ret(j):
        rs = sub_rows[j]
        if chain:
            return _retention_tile(rq[rs], rk[rs], rv[rs], rld, own(sr_ref, j))
        return _chunked_linear_attention(rq[rs], rk[rs], rv[rs], t_in_chunk * rld, chunk, chain, sr_ref)

    o_gla = stack_sub_tiles([gla(j) for j in sub_tiles])
    o_ret = stack_sub_tiles([ret(j) for j in sub_tiles])

    xq = proj_b(B_XQ, V_W)
    if chain:
        o_xa = stack_sub_tiles([_memory_attention_shared(xq[j * rows:(j + 1) * rows], mk_ref.at[pl.ds(j, 1)],
                                                         mv_ref.at[pl.ds(j, 1)]) for j in range(seqs)])
    else:
        o_xa = _memory_attention_per_seq(xq, chunk, mk_ref, mv_ref)

    gon = gon_ref[...]
    rgn = rgn_ref[...]
    for h in range(HEADS):
        hs = slice(h * DV, (h + 1) * DV)
        og = o_gla[h]
        o_gla[h] = og * lax.rsqrt(jnp.mean(og * og, axis=-1, keepdims=True) + NORM_EPS) * gon[:, hs]
        oc = o_ret[h] - jnp.mean(o_ret[h], axis=-1, keepdims=True)
        o_ret[h] = oc * lax.rsqrt(jnp.mean(oc * oc, axis=-1, keepdims=True) + GROUP_NORM_EPS) * rgn[:, hs]
    o_gla = (jnp.concatenate(o_gla, axis=1) * jax.nn.silu(proj_a(A_GR, V_W))).astype(BF16)
    o_ret = (jnp.concatenate(o_ret, axis=1) * jax.nn.silu(proj_b(B_RG, V_W))).astype(BF16)
    o_xa = jnp.concatenate(o_xa, axis=1).astype(BF16)

    def gate(j):
        return jax.nn.sigmoid(proj_b(B_GATES + j * D_MODEL, D_MODEL) + bg_ref[:, j * D_MODEL:(j + 1) * D_MODEL])

    merged = (gate(0) * _dot(o_gla, wbg_ref[...]) + gate(1) * _dot(o_ret, wbr_ref[...])
              + gate(2) * _dot(o_xa, wbx_ref[...]))
    xo_ref[...] = (x + _dot(merged.astype(BF16), wout_ref[...])).reshape(seqs, rows, D_MODEL)


def _ffn_kernel(x_ref, cb_in_ref, cb_buf_ref, n2g_ref, wup_ref, cw_ref, cbias_ref, wdown_ref, fg_ref,
                xo_ref, cb_ref, *, final_norm, layer):
    del cb_buf_ref
    n = x_ref.shape[0]
    cb_later, cb_ref = _split_stacked(cb_ref, layer)
    sb = cb_ref.shape[0]
    rows = n // sb

    @pl.when(pl.program_id(1) == 0)
    def _():
        cb_ref[...] = cb_in_ref[...]
        _zero_fill(cb_later)

    x = x_ref[...]
    hb = _rms_norm(x, n2g_ref[...]).astype(BF16)
    a = _dot(hb, wup_ref[:, :D_FF])
    gate_in = _dot(hb, wup_ref[:, D_FF:])

    cb = cb_ref[...]
    buf0 = jnp.broadcast_to(cb[:, 0:1, :], (sb, rows, D_FF)).reshape(n, D_FF)
    buf1 = jnp.broadcast_to(cb[:, 1:2, :], (sb, rows, D_FF)).reshape(n, D_FF)
    t = lax.broadcasted_iota(jnp.int32, (n, 1), 0) % rows
    prev1 = jnp.where(t == 0, buf1, pltpu.roll(a, 1, 0))
    prev2 = jnp.where(t == 0, buf0, jnp.where(t == 1, buf1, pltpu.roll(a, 2, 0)))
    cw = cw_ref[...]
    c = cbias_ref[...] + prev2 * cw[0:1] + prev1 * cw[1:2] + a * cw[2:3]
    cb_ref[...] = a.reshape(sb, rows, D_FF)[:, rows - (CONV_W - 1):, :]

    y = x + _dot((jax.nn.gelu(c) * gate_in).astype(BF16), wdown_ref[...])
    if final_norm:
        y = _rms_norm(y, fg_ref[...])
    xo_ref[...] = y


def _memkv_kernel(m_ref, g_ref, w_ref, k_ref, v_ref, k_rows_ref, v_rows_ref):
    tile = m_ref.shape[0]
    hb = _rms_norm(m_ref[...], g_ref[...]).astype(BF16)
    for col0, slot_ref, rows_ref in ((0, k_ref, k_rows_ref), (V_W, v_ref, v_rows_ref)):
        val = _dot(hb, w_ref[:, col0:col0 + V_W])
        slot_ref[...] = val
        for h in range(HEADS):
            rows_ref[pl.ds(h, tile, stride=HEADS), :] = val[:, h * DV:(h + 1) * DV]


def _layer_resident(a, layer):
    zeros = (0,) * (a.ndim - 1)
    return pl.BlockSpec((None,) + a.shape[1:], lambda *_: (layer,) + zeros, pipeline_mode=pl.Buffered(1))


def _layer_blocks(a, layer, sb):
    zeros = (0,) * (a.ndim - 2)
    return pl.BlockSpec((None, sb) + a.shape[2:], lambda i, l: (layer, i) + zeros)


def _stacked_out_blocks(a, layer, sb):
    if layer > 0:
        return _layer_blocks(a, layer, sb)
    zeros = (0,) * (a.ndim - 2)
    return pl.BlockSpec((a.shape[0], sb) + a.shape[2:], lambda i, l: (0, i) + zeros)


def _compiler_params():
    return pltpu.CompilerParams(dimension_semantics=("arbitrary", "arbitrary"),
                                vmem_limit_bytes=VMEM_LIMIT_BYTES)


_ANY = pl.BlockSpec(memory_space=pl.ANY)


def _mixer_call(x, cos, sin, mk, mv, sg_in, sr_in, sg_buf, sr_buf, w, *, layer, sb, rows, sub, chunk, chain,
                name):
    nseq, length, _ = x.shape
    x_spec = pl.BlockSpec((sb, rows, D_MODEL), lambda i, l: (i, l, 0))
    pos_spec = pl.BlockSpec((rows, QK_W), lambda i, l: (l, 0))
    weights = (w['norm1_g'], w['w_in_a'], w['w_in_b'], w['w_in_ga'], w['gla_a_up'], w['gla_a_b'],
               w['gla_onorm_g'], w['ret_gnorm_g'], w['ret_log_decay'], w['w_br_gla'], w['w_br_ret'],
               w['w_br_xa'], w['b_gate'], w['w_out'])
    return pl.pallas_call(
        functools.partial(_mixer_kernel, chunk=chunk, chain=chain, sub=sub, layer=layer),
        grid=(nseq // sb, length // rows),
        in_specs=[x_spec, pos_spec, pos_spec, _layer_blocks(mk, layer, sb), _layer_blocks(mv, layer, sb),
                  _layer_blocks(sg_in, layer, sb), _layer_blocks(sr_in, layer, sb), _ANY, _ANY]
                 + [_layer_resident(a, layer) for a in weights],
        out_specs=[x_spec, _stacked_out_blocks(sg_buf, layer, sb), _stacked_out_blocks(sr_buf, layer, sb)],
        out_shape=[jax.ShapeDtypeStruct(x.shape, F32), jax.ShapeDtypeStruct(sg_buf.shape, F32),
                   jax.ShapeDtypeStruct(sr_buf.shape, F32)],
        input_output_aliases={7: 1, 8: 2} if layer > 0 else {},
        compiler_params=_compiler_params(),
        name=name,
    )(x, cos, sin, mk, mv, sg_in, sr_in, sg_buf, sr_buf, *weights)


def _ffn_call(x, cb_in, cb_buf, w, *, layer, n, sb, nl, final_norm, name):
    tokens = x.shape[0]
    nb = tokens // (n * nl)
    x_spec = pl.BlockSpec((n, D_MODEL), lambda i, l: (i * nl + l, 0))
    weights = (w['norm2_g'], w['w_up'], w['conv_w'], w['conv_b'], w['w_down'], w['final_g'])
    return pl.pallas_call(
        functools.partial(_ffn_kernel, final_norm=final_norm, layer=layer),
        grid=(nb, nl),
        in_specs=[x_spec, _layer_blocks(cb_in, layer, sb), _ANY] + [_layer_resident(a, layer) for a in weights],
        out_specs=[x_spec, _stacked_out_blocks(cb_buf, layer, sb)],
        out_shape=[jax.ShapeDtypeStruct(x.shape, F32), jax.ShapeDtypeStruct(cb_buf.shape, F32)],
        input_output_aliases={2: 1} if layer > 0 else {},
        compiler_params=_compiler_params(),
        name=name,
    )(x, cb_in, cb_buf, *weights)


def _transpose_cast_kernel(wt_ref, o_ref):
    o_ref[...] = wt_ref[0].T.astype(BF16)


def _in_proj_section(w_in_t, first, cols, block):
    return pl.pallas_call(
        _transpose_cast_kernel,
        grid=(DEPTH, cols // block),
        in_specs=[pl.BlockSpec((pl.Element(1), pl.Element(block), pl.Element(D_MODEL)),
                               lambda d, j: (d, pl.multiple_of(first + j * block, SUBLANES), 0))],
        out_specs=pl.BlockSpec((None, D_MODEL, block), lambda d, j: (d, 0, j)),
        out_shape=jax.ShapeDtypeStruct((DEPTH, D_MODEL, cols), BF16),
        compiler_params=_compiler_params(),
        name=f"in_proj_cols_{first}",
    )(w_in_t)


def _memkv_call(mem, g, w):
    rows = mem.shape[0]
    tile = 512
    out = jax.ShapeDtypeStruct((DEPTH, rows, V_W), F32)
    out_rows = jax.ShapeDtypeStruct((DEPTH, rows * HEADS, DV), F32)
    kv_spec = pl.BlockSpec((None, tile, V_W), lambda d, r: (d, r, 0))
    kv_rows_spec = pl.BlockSpec((None, tile * HEADS, DV), lambda d, r: (d, r, 0))
    return pl.pallas_call(
        _memkv_kernel,
        grid=(DEPTH, rows // tile),
        in_specs=[pl.BlockSpec((tile, D_MODEL), lambda d, r: (r, 0)),
                  pl.BlockSpec((None, 1, D_MODEL), lambda d, r: (d, 0, 0)),
                  pl.BlockSpec((None, D_MODEL, 2 * V_W), lambda d, r: (d, 0, 0))],
        out_specs=[kv_spec, kv_spec, kv_rows_spec, kv_rows_spec],
        out_shape=[out, out, out_rows, out_rows],
        compiler_params=_compiler_params(),
        name="memory_kv",
    )(mem, g, w)


def _rope_tables(pos):
    half = DK // 2
    inv = ROPE_BASE ** (-jnp.arange(half, dtype=F32) / half)
    ang = pos.astype(F32)[:, None] * inv[None, :]
    cos = jnp.cos(ang)
    sin = jnp.sin(ang)
    return (jnp.tile(jnp.concatenate([cos, cos], axis=-1), (1, HEADS)),
            jnp.tile(jnp.concatenate([-sin, sin], axis=-1), (1, HEADS)))


def kernel(x_prompt, x_sample, mem_prompt, state_gla, state_ret, state_ffn_conv, cache_mem_k, cache_mem_v, norm1_g, w_in, gla_a_up, gla_a_b, gla_onorm_g, ret_gnorm_g, mem_norm_g, w_mem_kv, w_br_gla, w_br_ret, w_br_xa, b_gate, w_out, norm2_g, w_up, conv_w, conv_b, w_down, final_g):
    bp, lp, _ = x_prompt.shape
    bs, ls, _ = x_sample.shape
    chunk_p = math.gcd(lp, MIX_CHUNK)
    chunk_s = math.gcd(ls, MIX_CHUNK)
    assert bp % PROMPT_SEQS == 0 and lp % PROMPT_TILE == 0
    assert PROMPT_TILE % PROMPT_SUB_TILE == 0 and PROMPT_SUB_TILE % chunk_p == 0
    assert lp % PROMPT_FFN_TILE == 0
    assert chunk_s == ls and bs % SAMPLE_SEQS == 0 and bs % SAMPLE_FFN_SEQS == 0 and ls >= CONV_W - 1

    w_in_t = jnp.swapaxes(w_in, 1, 2)
    w = {
        'norm1_g': norm1_g[:, None, :],
        'w_in_a': _in_proj_section(w_in_t, 0, A_COLS, IN_PROJ_PREP_BLOCK),
        'w_in_b': _in_proj_section(w_in_t, A_COLS + GLA_LOWRANK, B_COLS, IN_PROJ_PREP_BLOCK),
        'w_in_ga': _in_proj_section(w_in_t, A_COLS, LANES, LANES),
        'gla_a_up': jnp.pad(gla_a_up.astype(BF16), ((0, 0), (0, LANES - GLA_LOWRANK), (0, 0))),
        'gla_a_b': gla_a_b[:, None, :], 'gla_onorm_g': gla_onorm_g[:, None, :],
        'ret_gnorm_g': ret_gnorm_g[:, None, :],
        'ret_log_decay': jnp.broadcast_to(
            jnp.repeat(jnp.log(1.0 - 2.0 ** (-5.0 - jnp.arange(HEADS, dtype=F32))), DK)[None, None, :],
            (DEPTH, 1, QK_W)),
        'w_br_gla': w_br_gla.astype(BF16), 'w_br_ret': w_br_ret.astype(BF16), 'w_br_xa': w_br_xa.astype(BF16),
        'b_gate': b_gate[:, None, :], 'w_out': w_out.astype(BF16),
        'norm2_g': norm2_g[:, None, :], 'w_up': w_up.astype(BF16), 'conv_w': conv_w,
        'conv_b': conv_b[:, None, :], 'w_down': w_down.astype(BF16),
        'final_g': jnp.broadcast_to(final_g[None, None, :], (DEPTH, 1, D_MODEL)),
    }

    cos_p, sin_p = _rope_tables(jnp.arange(lp, dtype=jnp.int32))
    cos_s, sin_s = _rope_tables(PAST_LEN + jnp.arange(ls, dtype=jnp.int32))

    mem_k, mem_v, mem_k_rows, mem_v_rows = _memkv_call(mem_prompt.reshape(bp * MEM_LEN, D_MODEL), mem_norm_g[:, None, :],
                               w_mem_kv.astype(BF16))
    mem_k = mem_k.reshape(DEPTH, bp, MEM_LEN, V_W)
    mem_v = mem_v.reshape(DEPTH, bp, MEM_LEN, V_W)
    cache_k = cache_mem_k.reshape(DEPTH, bs, MEM_LEN * HEADS, DV)
    cache_v = cache_mem_v.reshape(DEPTH, bs, MEM_LEN * HEADS, DV)

    pair_shape = (HEADS // 2, LANES, LANES)
    p_zero_state = jnp.zeros((DEPTH, bp) + pair_shape, F32)
    p_zero_conv = jnp.zeros((DEPTH, bp, CONV_W - 1, D_FF), F32)
    s_gla_in = state_gla.reshape((DEPTH, bs) + pair_shape)
    s_ret_in = state_ret.reshape((DEPTH, bs) + pair_shape)
    p_gla, p_ret, p_conv = p_zero_state, p_zero_state, p_zero_conv
    s_gla, s_ret, s_conv = s_gla_in, s_ret_in, state_ffn_conv

    xp, xs = x_prompt, x_sample
    for i in range(DEPTH):
        last = i == DEPTH - 1
        xp, p_gla, p_ret = _mixer_call(xp, cos_p, sin_p, mem_k, mem_v, p_zero_state, p_zero_state, p_gla, p_ret, w,
                                       layer=i, sb=PROMPT_SEQS, rows=PROMPT_TILE, sub=PROMPT_SUB_TILE, chunk=chunk_p,
                                       chain=True, name=f"mixer_prompt_{i}")
        xp, p_conv = _ffn_call(xp.reshape(bp * lp, D_MODEL), p_zero_conv, p_conv, w, layer=i, n=PROMPT_FFN_TILE,
                               sb=1, nl=lp // PROMPT_FFN_TILE, final_norm=last, name=f"ffn_prompt_{i}")
        xp = xp.reshape(bp, lp, D_MODEL)
        xs, s_gla, s_ret = _mixer_call(xs, cos_s, sin_s, cache_k, cache_v, s_gla_in, s_ret_in, s_gla, s_ret, w,
                                       layer=i, sb=SAMPLE_SEQS, rows=ls, sub=ls, chunk=chunk_s, chain=False,
                                       name=f"mixer_sample_{i}")
        xs, s_conv = _ffn_call(xs.reshape(bs * ls, D_MODEL), state_ffn_conv, s_conv, w, layer=i,
                               n=SAMPLE_FFN_SEQS * ls, sb=SAMPLE_FFN_SEQS, nl=1, final_norm=last,
                               name=f"ffn_sample_{i}")
        xs = xs.reshape(bs, ls, D_MODEL)

    kv_shape = (DEPTH, bp, MEM_LEN, HEADS, DV)
    return (xp, xs,
            p_gla.reshape(DEPTH, bp, HEADS, DK, DV), p_ret.reshape(DEPTH, bp, HEADS, DK, DV), p_conv,
            mem_k_rows.reshape(kv_shape), mem_v_rows.reshape(kv_shape),
            s_gla.reshape(DEPTH, bs, HEADS, DK, DV), s_ret.reshape(DEPTH, bs, HEADS, DK, DV), s_conv)
```

```python
import functools
import math

import jax
import jax.numpy as jnp
from jax import lax
from jax.experimental import pallas as pl
from jax.experimental.pallas import tpu as pltpu

F32 = jnp.float32
BF16 = jnp.bfloat16

D_MODEL = 1024
DEPTH = 2
PAST_LEN = 16384
HEADS = 4
DK = 64
DV = 128
QK_W = HEADS * DK
V_W = HEADS * DV
GLA_LOWRANK = 16
GLA_TAU = 16.0
ROPE_BASE = 10000.0
MIX_CHUNK = 32
MEM_LEN = 256
D_FF = 2816
CONV_W = 3
NORM_EPS = 1e-6
GROUP_NORM_EPS = 1e-5
N_BRANCH = 3

LANES = 128
SUBLANES = 8
VMEM_LIMIT_BYTES = 56 * 1024 * 1024

A_GQ = 0
A_GK = A_GQ + QK_W
A_GV = A_GK + QK_W
A_GR = A_GV + V_W
A_COLS = A_GR + V_W
B_RQ = 0
B_RK = B_RQ + QK_W
B_RV = B_RK + QK_W
B_RG = B_RV + V_W
B_XQ = B_RG + V_W
B_GATES = B_XQ + V_W
B_COLS = B_GATES + N_BRANCH * D_MODEL

PROMPT_SEQS = 1
PROMPT_TILE = 1024
PROMPT_SUB_TILE = 256
PROMPT_FFN_TILE = 1024
SAMPLE_SEQS = 8
SAMPLE_FFN_SEQS = 64
IN_PROJ_PREP_BLOCK = 512


def _dot(a, b):
    return jnp.dot(a, b, preferred_element_type=F32)


def _dot_nt(a, b):
    return lax.dot_general(a, b, (((1,), (1,)), ((), ())), preferred_element_type=F32)


def _rms_norm(x, g):
    return x * lax.rsqrt(jnp.mean(x * x, axis=-1, keepdims=True) + NORM_EPS) * g


def _split3(x):
    hi = x.astype(BF16)
    r = x - hi.astype(F32)
    mid = r.astype(BF16)
    lo = (r - mid.astype(F32)).astype(BF16)
    return hi, mid, lo


def _pad_rows(x, rows):
    if x.shape[0] == rows:
        return x
    return jnp.concatenate([x, jnp.zeros((rows - x.shape[0], x.shape[1]), x.dtype)], axis=0)


def _chunk_last(b, chunk):
    n, w = b.shape
    return b.reshape(n // chunk, chunk, w)[:, chunk - 1:chunk, :]


def _half_masks():
    lane = lax.broadcasted_iota(jnp.int32, (1, LANES), 1)
    return (lane < DK, lane >= DK)


def _pair_blocks(u):
    sub = lax.broadcasted_iota(jnp.int32, (LANES, DV), 0)
    return jnp.where(sub < DK, u[:, :DV], u[:, DV:])


def _chunked_linear_attention(q, k, v, b, chunk, chain, state_ref):
    n = q.shape[0]
    nc = n // chunk
    npad = max(n, LANES)
    last = _chunk_last(b, chunk)
    bl = jnp.broadcast_to(last, (nc, chunk, b.shape[1])).reshape(n, b.shape[1])
    last = last.reshape(nc, b.shape[1])
    q_dec = q * jnp.exp(b)
    k_dec = k * jnp.exp(-b)
    k_end = k * jnp.exp(bl - b)
    vb = v.astype(BF16)
    vb_pad = _pad_rows(vb, npad)

    row = lax.broadcasted_iota(jnp.int32, (n, n), 0)
    col = lax.broadcasted_iota(jnp.int32, (n, n), 1)
    causal = ((row // chunk) == (col // chunk)) & (col <= row)
    half_masks = _half_masks()
    lane_t = lax.broadcasted_iota(jnp.int32, (LANES, npad), 1)

    pairs = range(HEADS // 2)
    chunks = range(nc)
    qm, intra, u_blk, dec = [], [], [], []
    for p in pairs:
        ps = slice(p * LANES, (p + 1) * LANES)
        qp = q_dec[:, ps]
        kpb = k_dec[:, ps].astype(BF16)
        qm.append([jnp.where(m, qp, 0.0) for m in half_masks])
        for half in range(2):
            hs = slice((2 * p + half) * DV, (2 * p + half + 1) * DV)
            s = _dot_nt(qm[p][half].astype(BF16), kpb)
            intra.append(_dot(jnp.where(causal, s, 0.0).astype(BF16), vb[:, hs]))
        ke_t = _pad_rows(k_end[:, ps], npad).T
        total_t = _pad_rows(last[:, ps], LANES).T
        vp = vb_pad[:, 2 * p * DV:(2 * p + 2) * DV]
        u_blk.append([])
        dec.append([])
        for c in chunks:
            in_chunk = (lane_t >= c * chunk) & (lane_t < (c + 1) * chunk)
            u_blk[p].append(_pair_blocks(_dot(jnp.where(in_chunk, ke_t, 0.0).astype(BF16), vp)))
            dec[p].append(jnp.exp(total_t[:, c:c + 1]))

    s_start = []
    for p in pairs:
        s_start.append([])
        if chain:
            s_cur = state_ref[0, p]
        for c in chunks:
            if not chain:
                s_cur = state_ref[c, p]
            s_start[p].append(s_cur.astype(BF16))
            s_cur = dec[p][c] * s_cur + u_blk[p][c]
            if not chain:
                state_ref[c, p] = s_cur
        if chain:
            state_ref[0, p] = s_cur

    outs = []
    for p in pairs:
        inter = ([], [])
        for c in chunks:
            rs = slice(c * chunk, (c + 1) * chunk)
            lhs = jnp.concatenate([qm[p][0][rs], qm[p][1][rs]], axis=0).astype(BF16)
            oi = _dot(lhs, s_start[p][c])
            inter[0].append(oi[:chunk])
            inter[1].append(oi[chunk:])
        outs.append(intra[2 * p] + jnp.concatenate(inter[0], axis=0))
        outs.append(intra[2 * p + 1] + jnp.concatenate(inter[1], axis=0))
    return outs


def _retention_tile(q, k, v, rld, state_ref):
    n = q.shape[0]
    t = lax.broadcasted_iota(jnp.int32, (n, 1), 0).astype(F32)
    q_dec = q * jnp.exp((t + 1.0) * rld)
    k_end = k * jnp.exp((float(n - 1) - t) * rld)
    vb = v.astype(BF16)
    row = lax.broadcasted_iota(jnp.int32, (n, n), 0)
    col = lax.broadcasted_iota(jnp.int32, (n, n), 1)
    causal = col <= row
    delta = (row - col).astype(F32)
    half_masks = _half_masks()
    sub = lax.broadcasted_iota(jnp.int32, (LANES, 1), 0)

    pair_lanes = [slice(p * LANES, (p + 1) * LANES) for p in range(HEADS // 2)]
    log_g = [rld[:, h * DK:h * DK + 1] for h in range(HEADS)]
    raw = [_dot_nt(jnp.where(half_masks[h % 2], q[:, pair_lanes[h // 2]], 0.0).astype(BF16),
                   k[:, pair_lanes[h // 2]].astype(BF16)) for h in range(HEADS)]
    scores = [(raw[h] * jnp.where(causal, jnp.exp(delta * log_g[h]), 0.0)).astype(BF16) for h in range(HEADS)]
    s_cur = [state_ref[0, p] for p in range(HEADS // 2)]
    s_bf = [s.astype(BF16) for s in s_cur]
    outs = [_dot(scores[h], vb[:, h * DV:(h + 1) * DV])
            + _dot(jnp.where(half_masks[h % 2], q_dec[:, pair_lanes[h // 2]], 0.0).astype(BF16), s_bf[h // 2])
            for h in range(HEADS)]
    for p in range(HEADS // 2):
        u = _dot(k_end[:, pair_lanes[p]].T.astype(BF16), vb[:, 2 * p * DV:(2 * p + 2) * DV])
        dec = jnp.where(sub < DK, jnp.exp(float(n) * log_g[2 * p]), jnp.exp(float(n) * log_g[2 * p + 1]))
        state_ref[0, p] = dec * s_cur[p] + _pair_blocks(u)
    return outs


def _softmax_rows(s):
    e = jnp.exp(s - jnp.max(s, axis=-1, keepdims=True))
    return e / jnp.sum(e, axis=-1, keepdims=True)


def _memory_attention_shared(xq, mk_ref, mv_ref):
    mkb = mk_ref[0].astype(BF16)
    mvb = mv_ref[0].astype(BF16)
    heads = [slice(h * DV, (h + 1) * DV) for h in range(HEADS)]
    scores = [_dot_nt(xq[:, hs].astype(BF16), mkb[:, hs]) * (DV ** -0.5) for hs in heads]
    probs = [_softmax_rows(s).astype(BF16) for s in scores]
    return [_dot(prob, mvb[:, hs]) for prob, hs in zip(probs, heads)]


def _memory_attention_per_seq(xq, rows, mk_ref, mv_ref):
    nseq = mk_ref.shape[0]
    r = lax.broadcasted_iota(jnp.int32, (HEADS * rows, MEM_LEN * HEADS), 0)
    c = lax.broadcasted_iota(jnp.int32, (HEADS * rows, MEM_LEN * HEADS), 1)
    own_head = (c % HEADS) == (r // rows)
    scores = []
    for i in range(nseq):
        qb = xq[i * rows:(i + 1) * rows]
        lhs = jnp.concatenate([qb[:, h * DV:(h + 1) * DV] for h in range(HEADS)], axis=0)
        scores.append(_dot_nt(lhs.astype(BF16), mk_ref[i].astype(BF16)) * (DV ** -0.5))
    probs = [_softmax_rows(jnp.where(own_head, s, -1e30)).astype(BF16) for s in scores]
    pv = [_dot(probs[i], mv_ref[i].astype(BF16)) for i in range(nseq)]
    return [jnp.concatenate([pv[i][h * rows:(h + 1) * rows] for i in range(nseq)], axis=0) for h in range(HEADS)]


def _split_stacked(out_ref, layer):
    if layer == 0:
        return out_ref.at[pl.ds(1, DEPTH - 1)], out_ref.at[0]
    return None, out_ref


def _zero_fill(ref):
    if ref is not None:
        ref[...] = jnp.zeros(ref.shape, ref.dtype)


def _mixer_kernel(x_ref, cos_ref, sin_ref, mk_ref, mv_ref, sg_in_ref, sr_in_ref, sg_buf_ref, sr_buf_ref,
                  n1g_ref, wa_ref, wb_ref, wga_ref, aup_ref, ab_ref, gon_ref, rgn_ref, rld_ref,
                  wbg_ref, wbr_ref, wbx_ref, bg_ref, wout_ref,
                  xo_ref, sg_ref, sr_ref, *, chunk, chain, sub, layer):
    del sg_buf_ref, sr_buf_ref
    seqs, rows, _ = x_ref.shape
    n = seqs * rows
    if not chain:
        sub = n
    sg_later, sg_ref = _split_stacked(sg_ref, layer)
    sr_later, sr_ref = _split_stacked(sr_ref, layer)

    @pl.when(pl.program_id(1) == 0)
    def _():
        sg_ref[...] = sg_in_ref[...]
        sr_ref[...] = sr_in_ref[...]
        _zero_fill(sg_later)
        _zero_fill(sr_later)

    x = x_ref[...].reshape(n, D_MODEL)
    hb = _rms_norm(x, n1g_ref[...]).astype(BF16)

    def proj_a(off, width):
        return _dot(hb, wa_ref[:, off:off + width])

    def proj_b(off, width):
        return _dot(hb, wb_ref[:, off:off + width])

    sub_tiles = range(n // sub)
    sub_rows = [slice(j * sub, (j + 1) * sub) for j in sub_tiles]

    def own(ref, j):
        return ref.at[pl.ds(j * sub // rows, 1)] if chain else ref

    def stack_sub_tiles(outs):
        return [jnp.concatenate([o[h] for o in outs], axis=0) for h in range(HEADS)]

    row = lax.broadcasted_iota(jnp.int32, (sub, sub), 0)
    col = lax.broadcasted_iota(jnp.int32, (sub, sub), 1)
    tri = (((row // chunk) == (col // chunk)) & (col <= row)).astype(BF16)
    ga = _dot(hb, wga_ref[...]).astype(BF16)
    z = _dot(ga, aup_ref[...]) + ab_ref[...]
    lc = (jnp.minimum(z, 0.0) - jnp.log1p(jnp.exp(-jnp.abs(z)))) / GLA_TAU
    gq = proj_a(A_GQ, QK_W) * (DK ** -0.5)
    gk = proj_a(A_GK, QK_W)
    gv = proj_a(A_GV, V_W)

    def gla(j):
        rs = sub_rows[j]
        parts = _split3(lc[rs])
        b = _dot(tri, parts[0]) + _dot(tri, parts[1]) + _dot(tri, parts[2])
        return _chunked_linear_attention(gq[rs], gk[rs], gv[rs], b, chunk, chain, own(sg_ref, j))

    lane = lax.broadcasted_iota(jnp.int32, (1, QK_W), 1)
    first_half = (lane % DK) < (DK // 2)
    cos = jnp.concatenate([cos_ref[...]] * seqs, axis=0)
    sin = jnp.concatenate([sin_ref[...]] * seqs, axis=0)

    def rope(t):
        rot = jnp.where(first_half, pltpu.roll(t, QK_W - DK // 2, 1), pltpu.roll(t, DK // 2, 1))
        return t * cos + rot * sin

    rq = rope(proj_b(B_RQ, QK_W))
    rk = rope(proj_b(B_RK, QK_W)) * (DK ** -0.5)
    rv = proj_b(B_RV, V_W)
    rld = rld_ref[...]
    t_in_chunk = (lax.broadcasted_iota(jnp.int32, (sub, 1), 0) % chunk + 1).astype(F32)

    def ret(j):
        rs = sub_rows[j]
        if chain:
            return _retention_tile(rq[rs], rk[rs], rv[rs], rld, own(sr_ref, j))
        return _chunked_linear_attention(rq[rs], rk[rs], rv[rs], t_in_chunk * rld, chunk, chain, sr_ref)

    o_gla = stack_sub_tiles([gla(j) for j in sub_tiles])
    o_ret = stack_sub_tiles([ret(j) for j in sub_tiles])

    xq = proj_b(B_XQ, V_W)
    if chain:
        o_xa = stack_sub_tiles([_memory_attention_shared(xq[j * rows:(j + 1) * rows], mk_ref.at[pl.ds(j, 1)],
                                                         mv_ref.at[pl.ds(j, 1)]) for j in range(seqs)])
    else:
        o_xa = _memory_attention_per_seq(xq, chunk, mk_ref, mv_ref)

    gon = gon_ref[...]
    rgn = rgn_ref[...]
    for h in range(HEADS):
        hs = slice(h * DV, (h + 1) * DV)
        og = o_gla[h]
        o_gla[h] = og * lax.rsqrt(jnp.mean(og * og, axis=-1, keepdims=True) + NORM_EPS) * gon[:, hs]
        oc = o_ret[h] - jnp.mean(o_ret[h], axis=-1, keepdims=True)
        o_ret[h] = oc * lax.rsqrt(jnp.mean(oc * oc, axis=-1, keepdims=True) + GROUP_NORM_EPS) * rgn[:, hs]
    o_gla = (jnp.concatenate(o_gla, axis=1) * jax.nn.silu(proj_a(A_GR, V_W))).astype(BF16)
    o_ret = (jnp.concatenate(o_ret, axis=1) * jax.nn.silu(proj_b(B_RG, V_W))).astype(BF16)
    o_xa = jnp.concatenate(o_xa, axis=1).astype(BF16)

    def gate(j):
        return jax.nn.sigmoid(proj_b(B_GATES + j * D_MODEL, D_MODEL) + bg_ref[:, j * D_MODEL:(j + 1) * D_MODEL])

    merged = (gate(0) * _dot(o_gla, wbg_ref[...]) + gate(1) * _dot(o_ret, wbr_ref[...])
              + gate(2) * _dot(o_xa, wbx_ref[...]))
    xo_ref[...] = (x + _dot(merged.astype(BF16), wout_ref[...])).reshape(seqs, rows, D_MODEL)


def _ffn_kernel(x_ref, cb_in_ref, cb_buf_ref, n2g_ref, wup_ref, cw_ref, cbias_ref, wdown_ref, fg_ref,
                xo_ref, cb_ref, *, final_norm, layer):
    del cb_buf_ref
    n = x_ref.shape[0]
    cb_later, cb_ref = _split_stacked(cb_ref, layer)
    sb = cb_ref.shape[0]
    rows = n // sb

    @pl.when(pl.program_id(1) == 0)
    def _():
        cb_ref[...] = cb_in_ref[...]
        _zero_fill(cb_later)

    x = x_ref[...]
    hb = _rms_norm(x, n2g_ref[...]).astype(BF16)
    a = _dot(hb, wup_ref[:, :D_FF])
    gate_in = _dot(hb, wup_ref[:, D_FF:])

    cb = cb_ref[...]
    buf0 = jnp.broadcast_to(cb[:, 0:1, :], (sb, rows, D_FF)).reshape(n, D_FF)
    buf1 = jnp.broadcast_to(cb[:, 1:2, :], (sb, rows, D_FF)).reshape(n, D_FF)
    t = lax.broadcasted_iota(jnp.int32, (n, 1), 0) % rows
    prev1 = jnp.where(t == 0, buf1, pltpu.roll(a, 1, 0))
    prev2 = jnp.where(t == 0, buf0, jnp.where(t == 1, buf1, pltpu.roll(a, 2, 0)))
    cw = cw_ref[...]
    c = cbias_ref[...] + prev2 * cw[0:1] + prev1 * cw[1:2] + a * cw[2:3]
    cb_ref[...] = a.reshape(sb, rows, D_FF)[:, rows - (CONV_W - 1):, :]

    y = x + _dot((jax.nn.gelu(c) * gate_in).astype(BF16), wdown_ref[...])
    if final_norm:
        y = _rms_norm(y, fg_ref[...])
    xo_ref[...] = y


def _memkv_kernel(m_ref, g_ref, w_ref, k_ref, v_ref, k_rows_ref, v_rows_ref):
    tile = m_ref.shape[0]
    hb = _rms_norm(m_ref[...], g_ref[...]).astype(BF16)
    for col0, slot_ref, rows_ref in ((0, k_ref, k_rows_ref), (V_W, v_ref, v_rows_ref)):
        val = _dot(hb, w_ref[:, col0:col0 + V_W])
        slot_ref[...] = val
        for h in range(HEADS):
            rows_ref[pl.ds(h, tile, stride=HEADS), :] = val[:, h * DV:(h + 1) * DV]


def _layer_resident(a, layer):
    zeros = (0,) * (a.ndim - 1)
    return pl.BlockSpec((None,) + a.shape[1:], lambda *_: (layer,) + zeros, pipeline_mode=pl.Buffered(1))


def _layer_blocks(a, layer, sb):
    zeros = (0,) * (a.ndim - 2)
    return pl.BlockSpec((None, sb) + a.shape[2:], lambda i, l: (layer, i) + zeros)


def _stacked_out_blocks(a, layer, sb):
    if layer > 0:
        return _layer_blocks(a, layer, sb)
    zeros = (0,) * (a.ndim - 2)
    return pl.BlockSpec((a.shape[0], sb) + a.shape[2:], lambda i, l: (0, i) + zeros)


def _compiler_params():
    return pltpu.CompilerParams(dimension_semantics=("arbitrary", "arbitrary"),
                                vmem_limit_bytes=VMEM_LIMIT_BYTES)


_ANY = pl.BlockSpec(memory_space=pl.ANY)


def _mixer_call(x, cos, sin, mk, mv, sg_in, sr_in, sg_buf, sr_buf, w, *, layer, sb, rows, sub, chunk, chain,
                name):
    nseq, length, _ = x.shape
    x_spec = pl.BlockSpec((sb, rows, D_MODEL), lambda i, l: (i, l, 0))
    pos_spec = pl.BlockSpec((rows, QK_W), lambda i, l: (l, 0))
    weights = (w['norm1_g'], w['w_in_a'], w['w_in_b'], w['w_in_ga'], w['gla_a_up'], w['gla_a_b'],
               w['gla_onorm_g'], w['ret_gnorm_g'], w['ret_log_decay'], w['w_br_gla'], w['w_br_ret'],
               w['w_br_xa'], w['b_gate'], w['w_out'])
    return pl.pallas_call(
        functools.partial(_mixer_kernel, chunk=chunk, chain=chain, sub=sub, layer=layer),
        grid=(nseq // sb, length // rows),
        in_specs=[x_spec, pos_spec, pos_spec, _layer_blocks(mk, layer, sb), _layer_blocks(mv, layer, sb),
                  _layer_blocks(sg_in, layer, sb), _layer_blocks(sr_in, layer, sb), _ANY, _ANY]
                 + [_layer_resident(a, layer) for a in weights],
        out_specs=[x_spec, _stacked_out_blocks(sg_buf, layer, sb), _stacked_out_blocks(sr_buf, layer, sb)],
        out_shape=[jax.ShapeDtypeStruct(x.shape, F32), jax.ShapeDtypeStruct(sg_buf.shape, F32),
                   jax.ShapeDtypeStruct(sr_buf.shape, F32)],
        input_output_aliases={7: 1, 8: 2} if layer > 0 else {},
        compiler_params=_compiler_params(),
        name=name,
    )(x, cos, sin, mk, mv, sg_in, sr_in, sg_buf, sr_buf, *weights)


def _ffn_call(x, cb_in, cb_buf, w, *, layer, n, sb, nl, final_norm, name):
    tokens = x.shape[0]
    nb = tokens // (n * nl)
    x_spec = pl.BlockSpec((n, D_MODEL), lambda i, l: (i * nl + l, 0))
    weights = (w['norm2_g'], w['w_up'], w['conv_w'], w['conv_b'], w['w_down'], w['final_g'])
    return pl.pallas_call(
        functools.partial(_ffn_kernel, final_norm=final_norm, layer=layer),
        grid=(nb, nl),
        in_specs=[x_spec, _layer_blocks(cb_in, layer, sb), _ANY] + [_layer_resident(a, layer) for a in weights],
        out_specs=[x_spec, _stacked_out_blocks(cb_buf, layer, sb)],
        out_shape=[jax.ShapeDtypeStruct(x.shape, F32), jax.ShapeDtypeStruct(cb_buf.shape, F32)],
        input_output_aliases={2: 1} if layer > 0 else {},
        compiler_params=_compiler_params(),
        name=name,
    )(x, cb_in, cb_buf, *weights)


def _transpose_cast_kernel(wt_ref, o_ref):
    o_ref[...] = wt_ref[0].T.astype(BF16)


def _in_proj_section(w_in_t, first, cols, block):
    return pl.pallas_call(
        _transpose_cast_kernel,
        grid=(DEPTH, cols // block),
        in_specs=[pl.BlockSpec((pl.Element(1), pl.Element(block), pl.Element(D_MODEL)),
                               lambda d, j: (d, pl.multiple_of(first + j * block, SUBLANES), 0))],
        out_specs=pl.BlockSpec((None, D_MODEL, block), lambda d, j: (d, 0, j)),
        out_shape=jax.ShapeDtypeStruct((DEPTH, D_MODEL, cols), BF16),
        compiler_params=_compiler_params(),
        name=f"in_proj_cols_{first}",
    )(w_in_t)


def _memkv_call(mem, g, w):
    rows = mem.shape[0]
    tile = 512
    out = jax.ShapeDtypeStruct((DEPTH, rows, V_W), F32)
    out_rows = jax.ShapeDtypeStruct((DEPTH, rows * HEADS, DV), F32)
    kv_spec = pl.BlockSpec((None, tile, V_W), lambda d, r: (d, r, 0))
    kv_rows_spec = pl.BlockSpec((None, tile * HEADS, DV), lambda d, r: (d, r, 0))
    return pl.pallas_call(
        _memkv_kernel,
        grid=(DEPTH, rows // tile),
        in_specs=[pl.BlockSpec((tile, D_MODEL), lambda d, r: (r, 0)),
                  pl.BlockSpec((None, 1, D_MODEL), lambda d, r: (d, 0, 0)),
                  pl.BlockSpec((None, D_MODEL, 2 * V_W), lambda d, r: (d, 0, 0))],
        out_specs=[kv_spec, kv_spec, kv_rows_spec, kv_rows_spec],
        out_shape=[out, out, out_rows, out_rows],
        compiler_params=_compiler_params(),
        name="memory_kv",
    )(mem, g, w)


def _rope_tables(pos):
    half = DK // 2
    inv = ROPE_BASE ** (-jnp.arange(half, dtype=F32) / half)
    ang = pos.astype(F32)[:, None] * inv[None, :]
    cos = jnp.cos(ang)
    sin = jnp.sin(ang)
    return (jnp.tile(jnp.concatenate([cos, cos], axis=-1), (1, HEADS)),
            jnp.tile(jnp.concatenate([-sin, sin], axis=-1), (1, HEADS)))


def kernel(x_prompt, x_sample, mem_prompt, state_gla, state_ret, state_ffn_conv, cache_mem_k, cache_mem_v, norm1_g, w_in, gla_a_up, gla_a_b, gla_onorm_g, ret_gnorm_g, mem_norm_g, w_mem_kv, w_br_gla, w_br_ret, w_br_xa, b_gate, w_out, norm2_g, w_up, conv_w, conv_b, w_down, final_g):
    bp, lp, _ = x_prompt.shape
    bs, ls, _ = x_sample.shape
    chunk_p = math.gcd(lp, MIX_CHUNK)
    chunk_s = math.gcd(ls, MIX_CHUNK)
    assert bp % PROMPT_SEQS == 0 and lp % PROMPT_TILE == 0
    assert PROMPT_TILE % PROMPT_SUB_TILE == 0 and PROMPT_SUB_TILE % chunk_p == 0
    assert lp % PROMPT_FFN_TILE == 0
    assert chunk_s == ls and bs % SAMPLE_SEQS == 0 and bs % SAMPLE_FFN_SEQS == 0 and ls >= CONV_W - 1

    w_in_t = jnp.swapaxes(w_in, 1, 2)
    w = {
        'norm1_g': norm1_g[:, None, :],
        'w_in_a': _in_proj_section(w_in_t, 0, A_COLS, IN_PROJ_PREP_BLOCK),
        'w_in_b': _in_proj_section(w_in_t, A_COLS + GLA_LOWRANK, B_COLS, IN_PROJ_PREP_BLOCK),
        'w_in_ga': _in_proj_section(w_in_t, A_COLS, LANES, LANES),
        'gla_a_up': jnp.pad(gla_a_up.astype(BF16), ((0, 0), (0, LANES - GLA_LOWRANK), (0, 0))),
        'gla_a_b': gla_a_b[:, None, :], 'gla_onorm_g': gla_onorm_g[:, None, :],
        'ret_gnorm_g': ret_gnorm_g[:, None, :],
        'ret_log_decay': jnp.broadcast_to(
            jnp.repeat(jnp.log(1.0 - 2.0 ** (-5.0 - jnp.arange(HEADS, dtype=F32))), DK)[None, None, :],
            (DEPTH, 1, QK_W)),
        'w_br_gla': w_br_gla.astype(BF16), 'w_br_ret': w_br_ret.astype(BF16), 'w_br_xa': w_br_xa.astype(BF16),
        'b_gate': b_gate[:, None, :], 'w_out': w_out.astype(BF16),
        'norm2_g': norm2_g[:, None, :], 'w_up': w_up.astype(BF16), 'conv_w': conv_w,
        'conv_b': conv_b[:, None, :], 'w_down': w_down.astype(BF16),
        'final_g': jnp.broadcast_to(final_g[None, None, :], (DEPTH, 1, D_MODEL)),
    }

    cos_p, sin_p = _rope_tables(jnp.arange(lp, dtype=jnp.int32))
    cos_s, sin_s = _rope_tables(PAST_LEN + jnp.arange(ls, dtype=jnp.int32))

    mem_k, mem_v, mem_k_rows, mem_v_rows = _memkv_call(mem_prompt.reshape(bp * MEM_LEN, D_MODEL), mem_norm_g[:, None, :],
                               w_mem_kv.astype(BF16))
    mem_k = mem_k.reshape(DEPTH, bp, MEM_LEN, V_W)
    mem_v = mem_v.reshape(DEPTH, bp, MEM_LEN, V_W)
    cache_k = cache_mem_k.reshape(DEPTH, bs, MEM_LEN * HEADS, DV)
    cache_v = cache_mem_v.reshape(DEPTH, bs, MEM_LEN * HEADS, DV)

    pair_shape = (HEADS // 2, LANES, LANES)
    p_zero_state = jnp.zeros((DEPTH, bp) + pair_shape, F32)
    p_zero_conv = jnp.zeros((DEPTH, bp, CONV_W - 1, D_FF), F32)
    s_gla_in = state_gla.reshape((DEPTH, bs) + pair_shape)
    s_ret_in = state_ret.reshape((DEPTH, bs) + pair_shape)
    p_gla, p_ret, p_conv = p_zero_state, p_zero_state, p_zero_conv
    s_gla, s_ret, s_conv = s_gla_in, s_ret_in, state_ffn_conv

    xp, xs = x_prompt, x_sample
    for i in range(DEPTH):
        last = i == DEPTH - 1
        xp, p_gla, p_ret = _mixer_call(xp, cos_p, sin_p, mem_k, mem_v, p_zero_state, p_zero_state, p_gla, p_ret, w,
                                       layer=i, sb=PROMPT_SEQS, rows=PROMPT_TILE, sub=PROMPT_SUB_TILE, chunk=chunk_p,
                                       chain=True, name=f"mixer_prompt_{i}")
        xp, p_conv = _ffn_call(xp.reshape(bp * lp, D_MODEL), p_zero_conv, p_conv, w, layer=i, n=PROMPT_FFN_TILE,
                               sb=1, nl=lp // PROMPT_FFN_TILE, final_norm=last, name=f"ffn_prompt_{i}")
        xp = xp.reshape(bp, lp, D_MODEL)
        xs, s_gla, s_ret = _mixer_call(xs, cos_s, sin_s, cache_k, cache_v, s_gla_in, s_ret_in, s_gla, s_ret, w,
                                       layer=i, sb=SAMPLE_SEQS, rows=ls, sub=ls, chunk=chunk_s, chain=False,
                                       name=f"mixer_sample_{i}")
        xs, s_conv = _ffn_call(xs.reshape(bs * ls, D_MODEL), state_ffn_conv, s_conv, w, layer=i,
                               n=SAMPLE_FFN_SEQS * ls, sb=SAMPLE_FFN_SEQS, nl=1, final_norm=last,
                               name=f"ffn_sample_{i}")
        xs = xs.reshape(bs, ls, D_MODEL)

    kv_shape = (DEPTH, bp, MEM_LEN, HEADS, DV)
    return (xp, xs,
            p_gla.reshape(DEPTH, bp, HEADS, DK, DV), p_ret.reshape(DEPTH, bp, HEADS, DK, DV), p_conv,
            mem_k_rows.reshape(kv_shape), mem_v_rows.reshape(kv_shape),
            s_gla.reshape(DEPTH, bs, HEADS, DK, DV), s_ret.reshape(DEPTH, bs, HEADS, DK, DV), s_conv)
```
